```python
import jax, jax.numpy as jnp
from jax import lax
import numpy as np

D_MODEL = 1024
BATCH = 4
SEQ = 4096
DEPTH = 2
DEC_BATCH = 32
DEC_SEQ = 16
PAST_LEN = 4096

CHUNK = 64
N_A_LAYERS = DEPTH // 2
N_B_LAYERS = DEPTH - N_A_LAYERS
CONV_W = 3
N_HEADS = 16
HEAD_DIM = D_MODEL // N_HEADS
ATTN_DIM = N_HEADS * HEAD_DIM
Q_BLOCK = 128
N_GROUPS = 4
EXPERTS_PER_GROUP = 8
TOP_K_IN_GROUP = 2
N_EXPERTS = N_GROUPS * EXPERTS_PER_GROUP
D_EXPERT = D_MODEL // 2
RMS_EPS = 1e-6
FORGET_BIAS_MEAN = 2.0

kernel_name = 'yoco_shortconv_fox_hiermoe_stream_step'


def rmsnorm(x, g):
    x32 = x.astype(jnp.float32)
    y = x32 * lax.rsqrt(jnp.mean(x32 * x32, axis=-1, keepdims=True) + RMS_EPS)
    return (y * g.astype(jnp.float32)).astype(x.dtype)


def short_conv_mixer(h, conv_state, w_in, conv_w, w_out):
    T = h.shape[1]
    b_gate, c_gate, u = jnp.split(h @ w_in, 3, axis=-1)
    cu = c_gate * u
    u_pad = jnp.concatenate([conv_state.astype(cu.dtype), cu], axis=1)
    y = sum(conv_w[i] * u_pad[:, i:i + T] for i in range(CONV_W))
    return (b_gate * y) @ w_out, u_pad[:, -(CONV_W - 1):]


def shared_kv(h, norm_kv, w_k, w_v, w_f, b_f):
    B, T, _ = h.shape
    s = rmsnorm(h, norm_kv)
    k = (s @ w_k).reshape(B, T, N_HEADS, HEAD_DIM)
    v = (s @ w_v).reshape(B, T, N_HEADS, HEAD_DIM)
    logf = jax.nn.log_sigmoid((s @ w_f + b_f).astype(jnp.float32))
    return k, v, logf


def fox_block(q, q_pos, c_q, k, v, c_k, k_pos):
    scores = jnp.einsum('bqhd,bshd->bhqs', q, k, preferred_element_type=jnp.float32) * (HEAD_DIM ** -0.5)
    decay = jnp.swapaxes(c_q, 1, 2)[..., :, None] - jnp.swapaxes(c_k, 1, 2)[..., None, :]
    mask = k_pos[None, :] <= q_pos[:, None]
    probs = jax.nn.softmax(jnp.where(mask, scores + decay, -jnp.inf), axis=-1)
    return jnp.einsum('bhqs,bshd->bqhd', probs.astype(v.dtype), v)


def fox_attention(q, q_pos, c_q, k, v, c_k, k_pos):
    B, T, H, Dh = q.shape
    if T <= Q_BLOCK:
        return fox_block(q, q_pos, c_q, k, v, c_k, k_pos)
    nb = T // Q_BLOCK
    blocks = (q.reshape(B, nb, Q_BLOCK, H, Dh).swapaxes(0, 1),
              q_pos.reshape(nb, Q_BLOCK),
              c_q.reshape(B, nb, Q_BLOCK, H).swapaxes(0, 1))
    out = lax.map(lambda blk: fox_block(blk[0], blk[1], blk[2], k, v, c_k, k_pos), blocks)
    return out.swapaxes(0, 1).reshape(B, T, H, Dh)


def hier_moe(h, w_group, b_group, w_router, b_router, w_gate, w_up, w_down):
    B, T, D = h.shape
    ht = h.reshape(B * T, D)
    g_prob = jax.nn.softmax((ht @ w_group + b_group).astype(jnp.float32), axis=-1)
    g_idx = jnp.argmax(g_prob, axis=-1)
    g_w = jnp.max(g_prob, axis=-1)
    e_logits = (ht @ w_router + b_router).astype(jnp.float32).reshape(-1, N_GROUPS, EXPERTS_PER_GROUP)
    e_logits = jnp.take_along_axis(e_logits, g_idx[:, None, None], axis=1)[:, 0]
    top_v, top_i = lax.top_k(jax.nn.softmax(e_logits, axis=-1), TOP_K_IN_GROUP)
    top_v = top_v / jnp.sum(top_v, axis=-1, keepdims=True)
    local_w = jnp.sum(jax.nn.one_hot(top_i, EXPERTS_PER_GROUP, dtype=jnp.float32) * top_v[..., None], axis=1)
    comb = (jax.nn.one_hot(g_idx, N_GROUPS, dtype=jnp.float32)[:, :, None] * local_w[:, None, :]
            * g_w[:, None, None]).reshape(-1, N_EXPERTS).astype(h.dtype)
    out = jnp.zeros_like(ht)
    for e in range(N_EXPERTS):
        hid = jax.nn.silu(ht @ w_gate[e]) * (ht @ w_up[e])
        out = out + comb[:, e:e + 1] * (hid @ w_down[e])
    return out.reshape(B, T, D)


def run_trunk(x, past, p):
    B, T, D = x.shape
    h = x
    conv_new = []
    for layer in range(DEPTH):
        if layer < N_A_LAYERS:
            conv_in = jnp.zeros((B, CONV_W - 1, D), x.dtype) if past is None else past[0][layer]
            y, st = short_conv_mixer(rmsnorm(h, p['norm_a'][layer]), conv_in,
                                     p['w_in_a'][layer], p['conv_w_a'][layer], p['w_out_a'][layer])
            conv_new.append(st)
            h = h + y
        else:
            if layer == N_A_LAYERS:
                k_new, v_new, logf_new = shared_kv(h, p['norm_kv'], p['w_k'], p['w_v'], p['w_f'], p['b_f'])
                if past is None:
                    k_all, v_all, logf_all, n_past = k_new, v_new, logf_new, 0
                else:
                    n_past = past[1].shape[1]
                    k_all = jnp.concatenate([past[1], k_new], axis=1)
                    v_all = jnp.concatenate([past[2], v_new], axis=1)
                    logf_all = jnp.concatenate([past[3].astype(jnp.float32), logf_new], axis=1)
                c_all = jnp.cumsum(logf_all, axis=1)
                k_pos = jnp.arange(n_past + T)
                q_pos = k_pos[n_past:]
                c_q = c_all[:, n_past:]
            j = layer - N_A_LAYERS
            q = (rmsnorm(h, p['norm_b'][j]) @ p['w_q_b'][j]).reshape(B, T, N_HEADS, HEAD_DIM)
            o = fox_attention(q, q_pos, c_q, k_all, v_all, c_all, k_pos)
            h = h + o.reshape(B, T, ATTN_DIM) @ p['w_o_b'][j]
        h = h + hier_moe(rmsnorm(h, p['norm_ffn'][layer]), p['w_group'][layer], p['b_group'][layer],
                         p['w_router'][layer], p['b_router'][layer], p['w_gate'][layer],
                         p['w_up'][layer], p['w_down'][layer])
    return rmsnorm(h, p['norm_final']), jnp.stack(conv_new), k_new, v_new, logf_new


def setup_inputs(seed: int = 0) -> dict:
    key = jax.random.key(seed)
    ks = jax.random.split(key, 32)

    def nrm(i, shape, scale=1.0):
        return jax.random.normal(ks[i], shape, jnp.float32) * scale

    D, HD, F = D_MODEL, ATTN_DIM, D_EXPERT
    return {
        'x_prompt': nrm(0, (BATCH, SEQ, D)),
        'x_sample': nrm(1, (DEC_BATCH, DEC_SEQ, D)),
        'state_conv': nrm(2, (N_A_LAYERS, DEC_BATCH, CONV_W - 1, D)),
        'cache_k': nrm(3, (DEC_BATCH, PAST_LEN, N_HEADS, HEAD_DIM)),
        'cache_v': nrm(4, (DEC_BATCH, PAST_LEN, N_HEADS, HEAD_DIM)),
        'cache_logf': jax.nn.log_sigmoid(FORGET_BIAS_MEAN + nrm(5, (DEC_BATCH, PAST_LEN, N_HEADS))),
        'norm_a': 1.0 + nrm(6, (N_A_LAYERS, D), 0.01),
        'w_in_a': nrm(7, (N_A_LAYERS, D, 3 * D), D ** -0.5),
        'conv_w_a': nrm(8, (N_A_LAYERS, CONV_W, D), CONV_W ** -0.5),
        'w_out_a': nrm(9, (N_A_LAYERS, D, D), D ** -0.5),
        'norm_kv': 1.0 + nrm(10, (D,), 0.01),
        'w_k': nrm(11, (D, HD), D ** -0.5),
        'w_v': nrm(12, (D, HD), D ** -0.5),
        'w_f': nrm(13, (D, N_HEADS), D ** -0.5),
        'b_f': FORGET_BIAS_MEAN + nrm(14, (N_HEADS,), 0.1),
        'norm_b': 1.0 + nrm(15, (N_B_LAYERS, D), 0.01),
        'w_q_b': nrm(16, (N_B_LAYERS, D, HD), D ** -0.5),
        'w_o_b': nrm(17, (N_B_LAYERS, HD, D), HD ** -0.5),
        'norm_ffn': 1.0 + nrm(18, (DEPTH, D), 0.01),
        'w_group': nrm(19, (DEPTH, D, N_GROUPS), D ** -0.5),
        'b_group': nrm(20, (DEPTH, N_GROUPS), 0.01),
        'w_router': nrm(21, (DEPTH, D, N_EXPERTS), D ** -0.5),
        'b_router': nrm(22, (DEPTH, N_EXPERTS), 0.01),
        'w_gate': nrm(23, (DEPTH, N_EXPERTS, D, F), D ** -0.5),
        'w_up': nrm(24, (DEPTH, N_EXPERTS, D, F), D ** -0.5),
        'w_down': nrm(25, (DEPTH, N_EXPERTS, F, D), F ** -0.5),
        'norm_final': 1.0 + nrm(26, (D,), 0.01),
    }


def reference(x_prompt, x_sample, state_conv, cache_k, cache_v, cache_logf,
              norm_a, w_in_a, conv_w_a, w_out_a, norm_kv, w_k, w_v, w_f, b_f,
              norm_b, w_q_b, w_o_b, norm_ffn, w_group, b_group, w_router, b_router,
              w_gate, w_up, w_down, norm_final):
    params = dict(norm_a=norm_a, w_in_a=w_in_a, conv_w_a=conv_w_a, w_out_a=w_out_a,
                  norm_kv=norm_kv, w_k=w_k, w_v=w_v, w_f=w_f, b_f=b_f,
                  norm_b=norm_b, w_q_b=w_q_b, w_o_b=w_o_b, norm_ffn=norm_ffn,
                  w_group=w_group, b_group=b_group, w_router=w_router, b_router=b_router,
                  w_gate=w_gate, w_up=w_up, w_down=w_down, norm_final=norm_final)
    y_prompt, conv_prompt, k_prompt, v_prompt, logf_prompt = run_trunk(x_prompt, None, params)
    y_sample, conv_sample, k_sample, v_sample, logf_sample = run_trunk(
        x_sample, (state_conv, cache_k, cache_v, cache_logf), params)
    return (y_prompt, y_sample, conv_prompt, conv_sample, k_prompt, v_prompt, logf_prompt,
            k_sample, v_sample, logf_sample)
```

```python
import functools

import jax
import jax.numpy as jnp
from jax import lax
from jax.experimental import pallas as pl
from jax.experimental.pallas import tpu as pltpu

F32 = jnp.float32
BF16 = jnp.bfloat16
I32 = jnp.int32

RMS_EPS = 1e-6
LANES = 128
SUBLANES = 8
ROUTE_TILE = 256
TOKEN_TILE = 256
ATTN_BLOCK = 256
CACHE_CHUNK = 1024
VMEM_LIMIT = 56 * 1024 * 1024


def _cparams(n_axes):
    return pltpu.CompilerParams(dimension_semantics=("arbitrary",) * n_axes,
                                vmem_limit_bytes=VMEM_LIMIT)


def _rms(x, g):
    return x * lax.rsqrt(jnp.mean(x * x, axis=-1, keepdims=True) + RMS_EPS) * g


def _dot(a, b):
    return jnp.dot(a, b, preferred_element_type=F32)


def _dot_nt(a, b):
    return lax.dot_general(a, b, (((1,), (1,)), ((), ())), preferred_element_type=F32)


def _const_spec(shape):
    nd = len(shape)
    return pl.BlockSpec(shape, lambda *_: (0,) * nd)


def _route_tail(h, g_ffn, wr_ref, b_rt, carry_ref, n_groups, epg):
    tm = h.shape[0]
    hn = _rms(h, g_ffn)
    a1 = hn.astype(BF16)
    a2 = (hn - a1.astype(F32)).astype(BF16)
    w1 = wr_ref[0]
    w2 = wr_ref[1]
    logits = _dot(a1, w1) + _dot(a1, w2) + _dot(a2, w1) + b_rt
    lane = lax.broadcasted_iota(I32, logits.shape, 1)
    lanef = lane.astype(F32)
    neg = -jnp.inf
    big = 1e9
    gl = jnp.where(lane < n_groups, logits, neg)
    gmax = jnp.max(gl, axis=-1, keepdims=True)
    g_idx = jnp.min(jnp.where(gl == gmax, lanef, big), axis=-1, keepdims=True)
    g_w = 1.0 / jnp.sum(jnp.exp(gl - gmax), axis=-1, keepdims=True)
    lo = n_groups + g_idx * epg
    el = jnp.where((lanef >= lo) & (lanef < lo + epg), logits, neg)
    m1 = jnp.max(el, axis=-1, keepdims=True)
    i1 = jnp.min(jnp.where(el == m1, lanef, big), axis=-1, keepdims=True)
    el2 = jnp.where(lanef == i1, neg, el)
    m2 = jnp.max(el2, axis=-1, keepdims=True)
    i2 = jnp.min(jnp.where(el2 == m2, lanef, big), axis=-1, keepdims=True)
    t = jnp.exp(m2 - m1)
    cw1 = g_w * (1.0 / (1.0 + t))
    cw2 = g_w * (t / (1.0 + t))
    sel1 = lanef == i1
    sel2 = lanef == i2
    oh = jnp.where(sel1 | sel2, 1.0, 0.0)
    row = lax.broadcasted_iota(I32, (tm, tm), 0)
    col = lax.broadcasted_iota(I32, (tm, tm), 1)
    tri = jnp.where(col < row, 1.0, 0.0).astype(BF16)
    carry = carry_ref[...]
    prefix = _dot(tri, oh.astype(BF16)) + carry
    rank1 = jnp.sum(jnp.where(sel1, prefix, 0.0), axis=-1, keepdims=True)
    rank2 = jnp.sum(jnp.where(sel2, prefix, 0.0), axis=-1, keepdims=True)
    carry_ref[...] = carry + jnp.sum(oh, axis=0, keepdims=True)
    slab = jnp.where(lane == 0, i1 - n_groups,
           jnp.where(lane == 1, i2 - n_groups,
           jnp.where(lane == 2, cw1,
           jnp.where(lane == 3, cw2,
           jnp.where(lane == 4, rank1,
           jnp.where(lane == 5, rank2, 0.0))))))
    return slab


def _mixer_kernel(*refs, tm, d, tiles_per_seq, seq_len, n_groups, epg):
    masked = tiles_per_seq == 0
    if masked:
        (x_ref, s1_ref, s2_ref, na_ref, win_ref, cw_ref, wout_ref, nf_ref, wr_ref, brt_ref, cin_ref,
         h_ref, slab_ref, cu_ref, cout_ref, cubuf, carry_ref) = refs
    else:
        (x_ref, na_ref, win_ref, cw_ref, wout_ref, nf_ref, wr_ref, brt_ref, cin_ref,
         h_ref, slab_ref, cu_ref, cout_ref, cubuf, carry_ref) = refs
    i = pl.program_id(0)

    @pl.when(i == 0)
    def _():
        carry_ref[...] = cin_ref[...]

    if masked:
        cubuf[0:SUBLANES, :] = jnp.zeros((SUBLANES, d), F32)
    else:
        @pl.when(i % tiles_per_seq == 0)
        def _():
            cubuf[0:SUBLANES, :] = jnp.zeros((SUBLANES, d), F32)

    x = x_ref[...]
    xn = _rms(x, na_ref[...]).astype(BF16)
    bcu = _dot(xn, win_ref[...])
    b_gate = bcu[:, 0:d]
    cu = bcu[:, d:2 * d] * bcu[:, 2 * d:3 * d]
    cubuf[SUBLANES:SUBLANES + tm, :] = cu
    prev1 = cubuf[SUBLANES - 1:SUBLANES - 1 + tm, :]
    prev2 = cubuf[SUBLANES - 2:SUBLANES - 2 + tm, :]
    if masked:
        r = lax.broadcasted_iota(I32, (tm, d), 0) % seq_len
        prev1 = jnp.where(r == 0, s1_ref[...], prev1)
        prev2 = jnp.where(r < 2, s2_ref[...], prev2)
        cu_ref[...] = cu
    else:
        tail = cubuf[tm:tm + SUBLANES, :]
        cubuf[0:SUBLANES, :] = tail
        cu_ref[...] = tail
    cw = cw_ref[...]
    y = cw[0:1, :] * prev2 + cw[1:2, :] * prev1 + cw[2:3, :] * cu
    h = x + _dot((b_gate * y).astype(BF16), wout_ref[...])
    h_ref[...] = h
    slab_ref[...] = _route_tail(h, nf_ref[...], wr_ref, brt_ref[...], carry_ref, n_groups, epg)
    cout_ref[...] = carry_ref[...]


def _mixer_call(x, prev_rows, seq_len, na, win, cw, wout, nf, wr, brt, counts_in, n_groups, epg):
    n, d = x.shape
    tm = min(TOKEN_TILE, n)
    n_tiles = n // tm
    masked = prev_rows is not None
    tiles_per_seq = 0 if masked else seq_len // tm
    tok = pl.BlockSpec((tm, d), lambda i: (i, 0))
    in_specs = [tok]
    args = [x]
    if masked:
        in_specs += [tok, tok]
        args += list(prev_rows)
    in_specs += [_const_spec(na.shape), _const_spec(win.shape), _const_spec(cw.shape), _const_spec(wout.shape),
                 _const_spec(nf.shape), _const_spec(wr.shape), _const_spec(brt.shape), _const_spec(counts_in.shape)]
    args += [na, win, cw, wout, nf, wr, brt, counts_in]
    cu_shape = (n, d) if masked else (n_tiles * SUBLANES, d)
    cu_spec = tok if masked else pl.BlockSpec((SUBLANES, d), lambda i: (i, 0))
    return pl.pallas_call(
        functools.partial(_mixer_kernel, tm=tm, d=d, tiles_per_seq=tiles_per_seq, seq_len=seq_len,
                          n_groups=n_groups, epg=epg),
        grid=(n_tiles,),
        in_specs=in_specs,
        out_specs=[tok, pl.BlockSpec((tm, LANES), lambda i: (i, 0)), cu_spec, _const_spec((1, LANES))],
        out_shape=[jax.ShapeDtypeStruct((n, d), F32), jax.ShapeDtypeStruct((n, LANES), F32),
                   jax.ShapeDtypeStruct(cu_shape, F32), jax.ShapeDtypeStruct((1, LANES), F32)],
        scratch_shapes=[pltpu.VMEM((tm + SUBLANES, d), F32), pltpu.VMEM((1, LANES), F32)],
        compiler_params=_cparams(1),
        name="mixer_a_masked" if masked else "mixer_a",
    )(*args)


def _dispatch_kernel(pos_ref, h_ref, xs_in, xs_out, sem, *, tm, n):
    del xs_in
    base = pl.program_id(0) * tm

    def row_copy(r, p):
        return pltpu.make_async_copy(h_ref.at[pl.ds(r, 1)], xs_out.at[pl.ds(p, 1)], sem)

    def issue(r, c):
        row_copy(r, pos_ref[base + r]).start()
        row_copy(r, pos_ref[n + base + r]).start()
        return c

    lax.fori_loop(0, tm, issue, 0)

    def drain(r, c):
        row_copy(0, 0).wait()
        row_copy(0, 0).wait()
        return c

    lax.fori_loop(0, tm, drain, 0)


def _dispatch_call(pos, h, xs):
    n, d = h.shape
    tm = min(TOKEN_TILE, n)
    grid_spec = pltpu.PrefetchScalarGridSpec(
        num_scalar_prefetch=1,
        grid=(n // tm,),
        in_specs=[pl.BlockSpec((tm, d), lambda i, p: (i, 0)), pl.BlockSpec(memory_space=pl.ANY)],
        out_specs=pl.BlockSpec(memory_space=pl.ANY),
        scratch_shapes=[pltpu.SemaphoreType.DMA(())],
    )
    return pl.pallas_call(
        functools.partial(_dispatch_kernel, tm=tm, n=n),
        grid_spec=grid_spec,
        out_shape=jax.ShapeDtypeStruct(xs.shape, xs.dtype),
        input_output_aliases={2: 0},
        compiler_params=_cparams(1),
        name="moe_dispatch",
    )(pos, h, xs)


def _experts_kernel(te_ref, nv_ref, xs_ref, nf_ref, wg_ref, wu_ref, wd_ref, ys_ref, wg_bf, wu_bf, wd_bf):
    i = pl.program_id(0)

    @pl.when(i < nv_ref[0])
    def _():
        prev = te_ref[jnp.maximum(i - 1, 0)]

        @pl.when((i == 0) | (te_ref[i] != prev))
        def _():
            wg_bf[...] = wg_ref[0].astype(BF16)
            wu_bf[...] = wu_ref[0].astype(BF16)
            wd_bf[...] = wd_ref[0].astype(BF16)

        x = _rms(xs_ref[...], nf_ref[...]).astype(BF16)
        g = _dot(x, wg_bf[...])
        u = _dot(x, wu_bf[...])
        hid = (g * (1.0 / (1.0 + jnp.exp(-g))) * u).astype(BF16)
        ys_ref[...] = _dot(hid, wd_bf[...])

    @pl.when(i >= nv_ref[0])
    def _():
        ys_ref[...] = jnp.zeros(ys_ref.shape, F32)


def _experts_call(tile_expert, n_valid, xs, nf, wg, wu, wd):
    rows, d = xs.shape
    f = wg.shape[-1]
    tr = ROUTE_TILE

    def row_map(i, te, nv):
        return (jnp.minimum(i, nv[0] - 1), 0)

    grid_spec = pltpu.PrefetchScalarGridSpec(
        num_scalar_prefetch=2,
        grid=(rows // tr,),
        in_specs=[pl.BlockSpec((tr, d), row_map),
                  pl.BlockSpec((1, d), lambda i, te, nv: (0, 0)),
                  pl.BlockSpec((1, d, f), lambda i, te, nv: (te[i], 0, 0)),
                  pl.BlockSpec((1, d, f), lambda i, te, nv: (te[i], 0, 0)),
                  pl.BlockSpec((1, f, d), lambda i, te, nv: (te[i], 0, 0))],
        out_specs=pl.BlockSpec((tr, d), lambda i, te, nv: (i, 0)),
        scratch_shapes=[pltpu.VMEM((d, f), BF16), pltpu.VMEM((d, f), BF16), pltpu.VMEM((f, d), BF16)],
    )
    return pl.pallas_call(
        _experts_kernel,
        grid_spec=grid_spec,
        out_shape=jax.ShapeDtypeStruct((rows, d), F32),
        compiler_params=_cparams(1),
        name="moe_experts",
    )(tile_expert, n_valid, xs, nf, wg, wu, wd)


def _gather_rows(pos_ref, ys_hbm, ya_buf, yb_buf, sem, *, tm, n):
    i = pl.program_id(0)
    nt = pl.num_programs(0)
    slot = i % 2

    def copies(tile, r, s):
        base = tile * tm
        p1 = pos_ref[base + r]
        p2 = pos_ref[n + base + r]
        return (pltpu.make_async_copy(ys_hbm.at[pl.ds(p1, 1)], ya_buf.at[s, pl.ds(r, 1)], sem.at[s]),
                pltpu.make_async_copy(ys_hbm.at[pl.ds(p2, 1)], yb_buf.at[s, pl.ds(r, 1)], sem.at[s]))

    def start_tile(tile, s):
        def issue(r, c):
            ca, cb = copies(tile, r, s)
            ca.start()
            cb.start()
            return c
        lax.fori_loop(0, tm, issue, 0)

    @pl.when(i == 0)
    def _():
        start_tile(0, 0)

    @pl.when(i + 1 < nt)
    def _():
        start_tile(i + 1, 1 - slot)

    def drain(r, c):
        ca, cb = copies(i, r, slot)
        ca.wait()
        cb.wait()
        return c

    lax.fori_loop(0, tm, drain, 0)
    return ya_buf[slot], yb_buf[slot]


def _combine(h, slab, ya, yb):
    return h + (slab[:, 2:3] * ya + slab[:, 3:4] * yb)


def _kvq_kernel(pos_ref, h_ref, slab_ref, ys_hbm, nkv_ref, wk_ref, wv_ref, wf_ref, bf_ref, nb_ref, wq_ref,
                h2_ref, k_ref, v_ref, kb_ref, vb_ref, lf_ref, q_ref, ya_buf, yb_buf, sem, *, tm, n, q_scale):
    ya, yb = _gather_rows(pos_ref, ys_hbm, ya_buf, yb_buf, sem, tm=tm, n=n)
    h2 = _combine(h_ref[...], slab_ref[...], ya, yb)
    h2_ref[...] = h2
    s = _rms(h2, nkv_ref[...]).astype(BF16)
    k = _dot(s, wk_ref[...])
    v = _dot(s, wv_ref[...])
    k_ref[...] = k
    v_ref[...] = v
    kb_ref[...] = k.astype(BF16)
    vb_ref[...] = v.astype(BF16)
    z = _dot(s, wf_ref[...]) + bf_ref[...]
    lf_ref[...] = -(jnp.maximum(-z, 0.0) + jnp.log1p(jnp.exp(-jnp.abs(z))))
    qn = _rms(h2, nb_ref[...]).astype(BF16)
    q_ref[...] = (_dot(qn, wq_ref[...]) * q_scale).astype(BF16)


def _kvq_call(pos, h, slab, ys, nkv, wk, wv, wf, bf, nb, wq, q_scale):
    n, d = h.shape
    hd = wk.shape[1]
    tm = min(TOKEN_TILE, n)
    tok = lambda w: pl.BlockSpec((tm, w), lambda i, p: (i, 0))
    cst = lambda a: pl.BlockSpec(a.shape, lambda i, p: (0,) * a.ndim)
    grid_spec = pltpu.PrefetchScalarGridSpec(
        num_scalar_prefetch=1,
        grid=(n // tm,),
        in_specs=[tok(d), tok(LANES), pl.BlockSpec(memory_space=pl.ANY),
                  cst(nkv), cst(wk), cst(wv), cst(wf), cst(bf), cst(nb), cst(wq)],
        out_specs=[tok(d), tok(hd), tok(hd), tok(hd), tok(hd), tok(LANES), tok(hd)],
        scratch_shapes=[pltpu.VMEM((2, tm, d), F32), pltpu.VMEM((2, tm, d), F32), pltpu.SemaphoreType.DMA((2,))],
    )
    return pl.pallas_call(
        functools.partial(_kvq_kernel, tm=tm, n=n, q_scale=q_scale),
        grid_spec=grid_spec,
        out_shape=[jax.ShapeDtypeStruct((n, d), F32), jax.ShapeDtypeStruct((n, hd), F32),
                   jax.ShapeDtypeStruct((n, hd), F32), jax.ShapeDtypeStruct((n, hd), BF16),
                   jax.ShapeDtypeStruct((n, hd), BF16), jax.ShapeDtypeStruct((n, LANES), F32),
                   jax.ShapeDtypeStruct((n, hd), BF16)],
        compiler_params=_cparams(1),
        name="combine_kvq",
    )(pos, h, slab, ys, nkv, wk, wv, wf, bf, nb, wq)


def _final_kernel(pos_ref, h_ref, slab_ref, ys_hbm, nfin_ref, y_ref, ya_buf, yb_buf, sem, *, tm, n):
    ya, yb = _gather_rows(pos_ref, ys_hbm, ya_buf, yb_buf, sem, tm=tm, n=n)
    y_ref[...] = _rms(_combine(h_ref[...], slab_ref[...], ya, yb), nfin_ref[...])


def _final_call(pos, h, slab, ys, nfin):
    n, d = h.shape
    tm = min(TOKEN_TILE, n)
    tok = lambda w: pl.BlockSpec((tm, w), lambda i, p: (i, 0))
    grid_spec = pltpu.PrefetchScalarGridSpec(
        num_scalar_prefetch=1,
        grid=(n // tm,),
        in_specs=[tok(d), tok(LANES), pl.BlockSpec(memory_space=pl.ANY),
                  pl.BlockSpec(nfin.shape, lambda i, p: (0, 0))],
        out_specs=tok(d),
        scratch_shapes=[pltpu.VMEM((2, tm, d), F32), pltpu.VMEM((2, tm, d), F32), pltpu.SemaphoreType.DMA((2,))],
    )
    return pl.pallas_call(
        functools.partial(_final_kernel, tm=tm, n=n),
        grid_spec=grid_spec,
        out_shape=jax.ShapeDtypeStruct((n, d), F32),
        compiler_params=_cparams(1),
        name="combine_final",
    )(pos, h, slab, ys, nfin)


def _cumsum_kernel(x_ref, c_ref, *, t):
    rb = x_ref.shape[0]
    row = lax.broadcasted_iota(I32, (LANES, LANES), 0)
    col = lax.broadcasted_iota(I32, (LANES, LANES), 1)
    tri = jnp.where(row <= col, 1.0, 0.0).astype(BF16)
    carry = jnp.zeros((rb, 1), F32)
    for j in range(t // LANES):
        blk = x_ref[:, j * LANES:(j + 1) * LANES]
        a1 = blk.astype(BF16)
        r1 = blk - a1.astype(F32)
        a2 = r1.astype(BF16)
        a3 = (r1 - a2.astype(F32)).astype(BF16)
        cs = _dot(a1, tri) + _dot(a2, tri) + _dot(a3, tri) + carry
        c_ref[:, j * LANES:(j + 1) * LANES] = cs
        carry = cs[:, LANES - 1:LANES]


def _cumsum_call(x):
    r, t = x.shape
    rb = min(r, 128)
    return pl.pallas_call(
        functools.partial(_cumsum_kernel, t=t),
        grid=(r // rb,),
        in_specs=[pl.BlockSpec((rb, t), lambda i: (i, 0))],
        out_specs=pl.BlockSpec((rb, t), lambda i: (i, 0)),
        out_shape=jax.ShapeDtypeStruct((r, t), F32),
        compiler_params=_cparams(1),
        name="logf_cumsum",
    )(x)


def _attn_prompt_kernel(q_ref, k_ref, v_ref, cq_ref, ck_ref, o_ref, *, blk, dh, nkv):
    hp = pl.program_id(1)
    qi = pl.program_id(2)
    q = q_ref[...]
    lane = lax.broadcasted_iota(I32, q.shape, 1)
    cq_all = cq_ref[...]
    lane_c = lax.broadcasted_iota(I32, cq_all.shape, 1)
    row = lax.broadcasted_iota(I32, (blk, blk), 0)
    col = lax.broadcasted_iota(I32, (blk, blk), 1)
    causal = col <= row
    outs = []
    for head in range(2):
        h = 2 * hp + head
        in_head = (lane >= head * dh) & (lane < (head + 1) * dh)
        qh = jnp.where(in_head, q, jnp.zeros_like(q))
        cq = jnp.sum(jnp.where(lane_c == h, cq_all, 0.0), axis=-1, keepdims=True)

        def scores(j, masked):
            start = pl.multiple_of(j * blk, blk)
            kb = k_ref[pl.ds(start, blk), :]
            ck = ck_ref[pl.ds(h * nkv + j, 1), :]
            s = _dot_nt(qh, kb) + (cq - ck)
            if masked:
                s = jnp.where(causal, s, -jnp.inf)
            return s, v_ref[pl.ds(start, blk), :]

        def update(carry, s, vb):
            m, l, acc = carry
            m_new = jnp.maximum(m, jnp.max(s, axis=-1, keepdims=True))
            p = jnp.exp(s - m_new)
            alpha = jnp.exp(m - m_new)
            l = alpha * l + jnp.sum(p, axis=-1, keepdims=True)
            acc = alpha * acc + _dot(p.astype(BF16), vb)
            return m_new, l, acc

        def body(j, carry):
            s, vb = scores(j, False)
            return update(carry, s, vb)

        init = (jnp.full((blk, 1), -jnp.inf, F32), jnp.zeros((blk, 1), F32), jnp.zeros((blk, 2 * dh), F32))
        carry = lax.fori_loop(0, qi, body, init)
        s, vb = scores(qi, True)
        m, l, acc = update(carry, s, vb)
        outs.append(acc / l)
    o_ref[...] = jnp.where(lane < dh, outs[0], outs[1]).astype(BF16)


def _attn_prompt_call(q, kb, vb, cq_slab, ck_rows, batch, seq, n_heads, dh):
    blk = min(ATTN_BLOCK, seq)
    nq = seq // blk
    hd = n_heads * dh
    return pl.pallas_call(
        functools.partial(_attn_prompt_kernel, blk=blk, dh=dh, nkv=nq),
        grid=(batch, n_heads // 2, nq),
        in_specs=[pl.BlockSpec((blk, 2 * dh), lambda b, hp, qi: (b * nq + qi, hp)),
                  pl.BlockSpec((seq, 2 * dh), lambda b, hp, qi: (b, hp)),
                  pl.BlockSpec((seq, 2 * dh), lambda b, hp, qi: (b, hp)),
                  pl.BlockSpec((blk, LANES), lambda b, hp, qi: (b * nq + qi, 0)),
                  pl.BlockSpec((n_heads * nq, blk), lambda b, hp, qi: (b, 0))],
        out_specs=pl.BlockSpec((blk, 2 * dh), lambda b, hp, qi: (b * nq + qi, hp)),
        out_shape=jax.ShapeDtypeStruct((batch * seq, hd), BF16),
        compiler_params=_cparams(3),
        name="attn_prompt",
    )(q, kb, vb, cq_slab, ck_rows)


def _attn_sample_kernel(q_ref, kc_ref, vc_ref, kn_ref, vn_ref, cq_ref, ck_ref, o_ref,
                        qbd, m_ref, l_ref, acc_ref, *, ts, n_heads, dh, nchunk):
    j = pl.program_id(1)
    r = n_heads * ts
    hd = n_heads * dh

    @pl.when(j == 0)
    def _():
        q = q_ref[...]
        qt = jnp.concatenate([q] * n_heads, axis=0)
        rr = lax.broadcasted_iota(I32, (r, hd), 0) // ts
        ll = lax.broadcasted_iota(I32, (r, hd), 1) // dh
        qbd[...] = jnp.where(rr == ll, qt, jnp.zeros_like(qt))
        m_ref[...] = jnp.full((r, 1), -jnp.inf, F32)
        l_ref[...] = jnp.zeros((r, 1), F32)
        acc_ref[...] = jnp.zeros((r, hd), F32)

    def expand(ck):
        w = ck.shape[1]
        return jnp.concatenate([jnp.broadcast_to(ck[h:h + 1, :], (ts, w)) for h in range(n_heads)], axis=0)

    def update(s, vb):
        m = m_ref[...]
        m_new = jnp.maximum(m, jnp.max(s, axis=-1, keepdims=True))
        p = jnp.exp(s - m_new)
        alpha = jnp.exp(m - m_new)
        l_ref[...] = alpha * l_ref[...] + jnp.sum(p, axis=-1, keepdims=True)
        acc_ref[...] = alpha * acc_ref[...] + _dot(p.astype(BF16), vb)
        m_ref[...] = m_new

    cq = cq_ref[...][:, 0:1]

    @pl.when(j < nchunk)
    def _():
        kb = kc_ref[...].astype(BF16)
        s = _dot_nt(qbd[...], kb) + (cq - expand(ck_ref[...]))
        update(s, vc_ref[...].astype(BF16))

    @pl.when(j == nchunk)
    def _():
        s = _dot_nt(qbd[...], kn_ref[...]) + (cq - expand(ck_ref[:, 0:ts]))
        qpos = lax.broadcasted_iota(I32, (r, ts), 0) % ts
        kpos = lax.broadcasted_iota(I32, (r, ts), 1)
        s = jnp.where(kpos <= qpos, s, -jnp.inf)
        update(s, vn_ref[...])
        o = acc_ref[...] / l_ref[...]
        ll = lax.broadcasted_iota(I32, (ts, hd), 1) // dh
        out = jnp.zeros((ts, hd), F32)
        for h in range(n_heads):
            out = jnp.where(ll == h, o[h * ts:(h + 1) * ts, :], out)
        o_ref[...] = out.astype(BF16)


def _attn_sample_call(q, cache_k, cache_v, k_new, v_new, cq_rep, ck_chunks, n_streams, ts, n_heads, dh):
    past = cache_k.shape[1]
    hd = n_heads * dh
    ck_len = min(CACHE_CHUNK, past)
    nchunk = past // ck_len
    r = n_heads * ts
    cache_spec = pl.BlockSpec((None, ck_len, hd), lambda b, j: (b, jnp.minimum(j, nchunk - 1), 0))
    new_spec = pl.BlockSpec((ts, hd), lambda b, j: (b, 0))
    return pl.pallas_call(
        functools.partial(_attn_sample_kernel, ts=ts, n_heads=n_heads, dh=dh, nchunk=nchunk),
        grid=(n_streams, nchunk + 1),
        in_specs=[new_spec, cache_spec, cache_spec, new_spec, new_spec,
                  pl.BlockSpec((r, LANES), lambda b, j: (b, 0)),
                  pl.BlockSpec((n_heads, ck_len), lambda b, j: (b * (nchunk + 1) + j, 0))],
        out_specs=new_spec,
        out_shape=jax.ShapeDtypeStruct((n_streams * ts, hd), BF16),
        scratch_shapes=[pltpu.VMEM((r, hd), BF16), pltpu.VMEM((r, 1), F32), pltpu.VMEM((r, 1), F32),
                        pltpu.VMEM((r, hd), F32)],
        compiler_params=_cparams(2),
        name="attn_sample",
    )(q, cache_k, cache_v, k_new, v_new, cq_rep, ck_chunks)


def _wo_kernel(h_ref, o_ref, wo_ref, nf_ref, wr_ref, brt_ref, cin_ref, h3_ref, slab_ref, cout_ref, carry_ref,
               *, n_groups, epg):
    @pl.when(pl.program_id(0) == 0)
    def _():
        carry_ref[...] = cin_ref[...]

    h3 = h_ref[...] + _dot(o_ref[...], wo_ref[...])
    h3_ref[...] = h3
    slab_ref[...] = _route_tail(h3, nf_ref[...], wr_ref, brt_ref[...], carry_ref, n_groups, epg)
    cout_ref[...] = carry_ref[...]


def _wo_call(h, o, wo, nf, wr, brt, counts_in, n_groups, epg):
    n, d = h.shape
    hd = o.shape[1]
    tm = min(TOKEN_TILE, n)
    return pl.pallas_call(
        functools.partial(_wo_kernel, n_groups=n_groups, epg=epg),
        grid=(n // tm,),
        in_specs=[pl.BlockSpec((tm, d), lambda i: (i, 0)), pl.BlockSpec((tm, hd), lambda i: (i, 0)),
                  _const_spec(wo.shape), _const_spec(nf.shape), _const_spec(wr.shape), _const_spec(brt.shape),
                  _const_spec(counts_in.shape)],
        out_specs=[pl.BlockSpec((tm, d), lambda i: (i, 0)), pl.BlockSpec((tm, LANES), lambda i: (i, 0)),
                   _const_spec((1, LANES))],
        out_shape=[jax.ShapeDtypeStruct((n, d), F32), jax.ShapeDtypeStruct((n, LANES), F32),
                   jax.ShapeDtypeStruct((1, LANES), F32)],
        scratch_shapes=[pltpu.VMEM((1, LANES), F32)],
        compiler_params=_cparams(1),
        name="wo_router",
    )(h, o, wo, nf, wr, brt, counts_in)


def _router_weights(w_group, b_group, w_router, b_router):
    d = w_group.shape[0]
    n = w_group.shape[1] + w_router.shape[1]
    w = jnp.zeros((d, LANES), F32).at[:, :n].set(jnp.concatenate([w_group, w_router], axis=1))
    w1 = w.astype(BF16)
    w2 = (w - w1.astype(F32)).astype(BF16)
    b = jnp.zeros((1, LANES), F32).at[0, :n].set(jnp.concatenate([b_group, b_router]))
    return jnp.stack([w1, w2]), b


def _route_tables(slabs, counts, n_groups, n_experts):
    tile = ROUTE_TILE
    n_all = sum(s.shape[0] for s in slabs)
    cnt = counts[0, n_groups:n_groups + n_experts].astype(I32)
    padded = ((cnt + tile - 1) // tile) * tile
    ends = jnp.cumsum(padded)
    offs = ends - padded
    n_tiles = (2 * n_all + n_experts * (tile - 1) + tile - 1) // tile
    n_valid = (ends[-1] // tile).astype(I32)
    starts = jnp.arange(n_tiles, dtype=I32) * tile
    te = jnp.minimum(jnp.sum(starts[:, None] >= ends[None, :], axis=1), n_experts - 1).astype(I32)
    te = jnp.where(jnp.arange(n_tiles) < n_valid, te, te[jnp.maximum(n_valid - 1, 0)])
    pos = []
    for s in slabs:
        e1 = s[:, 0].astype(I32)
        e2 = s[:, 1].astype(I32)
        pos.append(jnp.concatenate([offs[e1] + s[:, 4].astype(I32), offs[e2] + s[:, 5].astype(I32)]))
    return pos, te, n_valid.reshape(1), n_tiles * tile


def _moe(hs, slabs, counts, nf, wg, wu, wd, n_groups, n_experts):
    pos, te, n_valid, rows = _route_tables(slabs, counts, n_groups, n_experts)
    xs = jnp.zeros((rows, hs[0].shape[1]), F32)
    for p, h in zip(pos, hs):
        xs = _dispatch_call(p, h, xs)
    ys = _experts_call(te, n_valid, xs, nf, wg, wu, wd)
    return pos, ys


def kernel(x_prompt, x_sample, state_conv, cache_k, cache_v, cache_logf, norm_a, w_in_a, conv_w_a, w_out_a,
           norm_kv, w_k, w_v, w_f, b_f, norm_b, w_q_b, w_o_b, norm_ffn, w_group, b_group, w_router, b_router,
           w_gate, w_up, w_down, norm_final):
    bp, tp, d = x_prompt.shape
    bs, ts, _ = x_sample.shape
    past, n_heads, dh = cache_k.shape[1], cache_k.shape[2], cache_k.shape[3]
    hd = n_heads * dh
    n_groups = w_group.shape[-1]
    n_experts = w_gate.shape[1]
    epg = n_experts // n_groups
    assert state_conv.shape[0] == 1 and w_q_b.shape[0] == 1 and state_conv.shape[2] == 2
    assert dh * 2 == LANES and n_heads % 2 == 0
    np_, ns = bp * tp, bs * ts

    row = lambda a: a.reshape(1, -1).astype(F32)
    xp = x_prompt.reshape(np_, d)
    xs_ = x_sample.reshape(ns, d)
    win = w_in_a[0].astype(BF16)
    wout = w_out_a[0].astype(BF16)
    wk, wv, wq, wo = w_k.astype(BF16), w_v.astype(BF16), w_q_b[0].astype(BF16), w_o_b[0].astype(BF16)
    wf = jnp.zeros((d, LANES), F32).at[:, :n_heads].set(w_f).astype(BF16)
    bf = jnp.zeros((1, LANES), F32).at[0, :n_heads].set(b_f)
    wr0, br0 = _router_weights(w_group[0], b_group[0], w_router[0], b_router[0])
    wr1, br1 = _router_weights(w_group[1], b_group[1], w_router[1], b_router[1])
    zero_counts = jnp.zeros((1, LANES), F32)

    st = state_conv[0]
    s1 = jnp.zeros((bs, ts, d), F32).at[:, 0].set(st[:, 1]).reshape(ns, d)
    s2 = jnp.zeros((bs, ts, d), F32).at[:, 0].set(st[:, 0]).at[:, 1].set(st[:, 1]).reshape(ns, d)

    mix = functools.partial(_mixer_call, na=row(norm_a[0]), win=win, cw=conv_w_a[0], wout=wout,
                            nf=row(norm_ffn[0]), wr=wr0, brt=br0, n_groups=n_groups, epg=epg)
    h1p, slab0p, cup, cnt = mix(xp, None, tp, counts_in=zero_counts)
    h1s, slab0s, cus, cnt = mix(xs_, (s1, s2), ts, counts_in=cnt)
    tiles_per_seq = tp // min(TOKEN_TILE, np_)
    conv_prompt = cup.reshape(bp, tiles_per_seq, SUBLANES, d)[:, -1, SUBLANES - 2:][None]
    conv_sample = cus.reshape(bs, ts, d)[:, ts - 2:][None]
    (pos0p, pos0s), ys0 = _moe([h1p, h1s], [slab0p, slab0s], cnt, row(norm_ffn[0]),
                               w_gate[0], w_up[0], w_down[0], n_groups, n_experts)

    kvq = functools.partial(_kvq_call, ys=ys0, nkv=row(norm_kv), wk=wk, wv=wv, wf=wf, bf=bf,
                            nb=row(norm_b[0]), wq=wq, q_scale=float(dh) ** -0.5)
    h2p, kp, vp, kbp, vbp, lfp, qp = kvq(pos0p, h1p, slab0p)
    h2s, ks, vs, kbs, vbs, lfs, qs = kvq(pos0s, h1s, slab0s)
    logf_prompt = lfp[:, :n_heads].reshape(bp, tp, n_heads)
    logf_sample = lfs[:, :n_heads].reshape(bs, ts, n_heads)

    c_p = _cumsum_call(logf_prompt.transpose(0, 2, 1).reshape(bp * n_heads, tp))
    blk = min(ATTN_BLOCK, tp)
    cq_p = jnp.zeros((np_, LANES), F32).at[:, :n_heads].set(
        c_p.reshape(bp, n_heads, tp).transpose(0, 2, 1).reshape(np_, n_heads))
    ck_p = c_p.reshape(bp * n_heads * (tp // blk), blk)
    o_p = _attn_prompt_call(qp, kbp, vbp, cq_p, ck_p, bp, tp, n_heads, dh)

    tall = past + ts
    tpad = ((tall + LANES - 1) // LANES) * LANES
    lf_all = jnp.concatenate([cache_logf.astype(F32), logf_sample], axis=1).transpose(0, 2, 1)
    lf_all = jnp.pad(lf_all, ((0, 0), (0, 0), (0, tpad - tall))).reshape(bs * n_heads, tpad)
    c_s = _cumsum_call(lf_all).reshape(bs, n_heads, tpad)
    ck_len = min(CACHE_CHUNK, past)
    nchunk = past // ck_len
    c_past = c_s[:, :, :past].reshape(bs, n_heads, nchunk, ck_len).transpose(0, 2, 1, 3)
    c_new = c_s[:, :, past:past + ts]
    c_new_pad = jnp.pad(c_new, ((0, 0), (0, 0), (0, ck_len - ts)))[:, None]
    ck_s = jnp.concatenate([c_past, c_new_pad], axis=1).reshape(bs * (nchunk + 1) * n_heads, ck_len)
    cq_s = jnp.broadcast_to(c_new.reshape(bs * n_heads * ts, 1), (bs * n_heads * ts, LANES))
    o_s = _attn_sample_call(qs, cache_k.reshape(bs, past, hd), cache_v.reshape(bs, past, hd), kbs, vbs,
                            cq_s, ck_s, bs, ts, n_heads, dh)

    wo_r = functools.partial(_wo_call, wo=wo, nf=row(norm_ffn[1]), wr=wr1, brt=br1, n_groups=n_groups, epg=epg)
    h3p, slab1p, cnt1 = wo_r(h2p, o_p, counts_in=zero_counts)
    h3s, slab1s, cnt1 = wo_r(h2s, o_s, counts_in=cnt1)
    (pos1p, pos1s), ys1 = _moe([h3p, h3s], [slab1p, slab1s], cnt1, row(norm_ffn[1]),
                               w_gate[1], w_up[1], w_down[1], n_groups, n_experts)
    y_p = _final_call(pos1p, h3p, slab1p, ys1, row(norm_final))
    y_s = _final_call(pos1s, h3s, slab1s, ys1, row(norm_final))

    return (y_p.reshape(bp, tp, d), y_s.reshape(bs, ts, d), conv_prompt, conv_sample,
            kp.reshape(bp, tp, n_heads, dh), vp.reshape(bp, tp, n_heads, dh), logf_prompt,
            ks.reshape(bs, ts, n_heads, dh), vs.reshape(bs, ts, n_heads, dh), logf_sample)
```

```python
import functools

import jax
import jax.numpy as jnp
from jax import lax
from jax.experimental import pallas as pl
from jax.experimental.pallas import tpu as pltpu

F32 = jnp.float32
BF16 = jnp.bfloat16
I32 = jnp.int32

RMS_EPS = 1e-6
LOG2E = 1.4426950408889634
LANES = 128
SUBLANES = 8
ROUTE_TILE = 256
TOKEN_TILE = 256
ATTN_BLOCK = 512
CACHE_CHUNK = 1024
DMA_UNROLL = 8
VMEM_LIMIT = 56 * 1024 * 1024


def _cparams(n_axes):
    return pltpu.CompilerParams(dimension_semantics=("arbitrary",) * n_axes,
                                vmem_limit_bytes=VMEM_LIMIT)


def _rms(x, g):
    return x * lax.rsqrt(jnp.mean(x * x, axis=-1, keepdims=True) + RMS_EPS) * g


def _dot(a, b):
    return jnp.dot(a, b, preferred_element_type=F32)


def _dot_nt(a, b):
    return lax.dot_general(a, b, (((1,), (1,)), ((), ())), preferred_element_type=F32)


def _const_spec(shape):
    nd = len(shape)
    return pl.BlockSpec(shape, lambda *_: (0,) * nd)


def _route_tail(h, g_ffn, wr_ref, b_rt, carry_ref, n_groups, epg):
    tm = h.shape[0]
    hn = _rms(h, g_ffn)
    a1 = hn.astype(BF16)
    a2 = (hn - a1.astype(F32)).astype(BF16)
    w1 = wr_ref[0]
    w2 = wr_ref[1]
    logits = _dot(a1, w1) + _dot(a1, w2) + _dot(a2, w1) + b_rt
    lane = lax.broadcasted_iota(I32, logits.shape, 1)
    lanef = lane.astype(F32)
    neg = -jnp.inf
    big = 1e9
    gl = jnp.where(lane < n_groups, logits, neg)
    gmax = jnp.max(gl, axis=-1, keepdims=True)
    g_idx = jnp.min(jnp.where(gl == gmax, lanef, big), axis=-1, keepdims=True)
    g_w = 1.0 / jnp.sum(jnp.exp(gl - gmax), axis=-1, keepdims=True)
    lo = n_groups + g_idx * epg
    el = jnp.where((lanef >= lo) & (lanef < lo + epg), logits, neg)
    m1 = jnp.max(el, axis=-1, keepdims=True)
    i1 = jnp.min(jnp.where(el == m1, lanef, big), axis=-1, keepdims=True)
    el2 = jnp.where(lanef == i1, neg, el)
    m2 = jnp.max(el2, axis=-1, keepdims=True)
    i2 = jnp.min(jnp.where(el2 == m2, lanef, big), axis=-1, keepdims=True)
    t = jnp.exp(m2 - m1)
    cw1 = g_w * (1.0 / (1.0 + t))
    cw2 = g_w * (t / (1.0 + t))
    sel1 = lanef == i1
    sel2 = lanef == i2
    oh = jnp.where(sel1 | sel2, 1.0, 0.0)
    row = lax.broadcasted_iota(I32, (tm, tm), 0)
    col = lax.broadcasted_iota(I32, (tm, tm), 1)
    tri = jnp.where(col < row, 1.0, 0.0).astype(BF16)
    carry = carry_ref[...]
    prefix = _dot(tri, oh.astype(BF16)) + carry
    rank1 = jnp.sum(jnp.where(sel1, prefix, 0.0), axis=-1, keepdims=True)
    rank2 = jnp.sum(jnp.where(sel2, prefix, 0.0), axis=-1, keepdims=True)
    carry_ref[...] = carry + jnp.sum(oh, axis=0, keepdims=True)
    slab = jnp.where(lane == 0, i1 - n_groups,
           jnp.where(lane == 1, i2 - n_groups,
           jnp.where(lane == 2, cw1,
           jnp.where(lane == 3, cw2,
           jnp.where(lane == 4, rank1,
           jnp.where(lane == 5, rank2, 0.0))))))
    return slab


def _mixer_kernel(xp_ref, xs_ref, s1_ref, s2_ref, na_ref, win_ref, cw_ref, wout_ref, nf_ref, wr_ref, brt_ref,
                  h_ref, slab_ref, tail_ref, cus_ref, cnt_ref, cubuf, carry_ref,
                  *, tm, d, p_tiles, tiles_per_seq, s_len, n_groups, epg):
    i = pl.program_id(0)
    sample = i >= p_tiles

    @pl.when(i == 0)
    def _():
        carry_ref[...] = jnp.zeros(carry_ref.shape, F32)

    @pl.when(sample | (i % tiles_per_seq == 0))
    def _():
        cubuf[0:SUBLANES, :] = jnp.zeros((SUBLANES, d), F32)

    x = jnp.where(sample, xs_ref[...], xp_ref[...])
    xn = _rms(x, na_ref[...]).astype(BF16)
    bcu = _dot(xn, win_ref[...])
    b_gate = bcu[:, 0:d]
    cu = bcu[:, d:2 * d] * bcu[:, 2 * d:3 * d]
    cubuf[SUBLANES:SUBLANES + tm, :] = cu
    prev1 = cubuf[SUBLANES - 1:SUBLANES - 1 + tm, :]
    prev2 = cubuf[SUBLANES - 2:SUBLANES - 2 + tm, :]
    r = lax.broadcasted_iota(I32, (tm, d), 0) % s_len
    prev1 = jnp.where(sample & (r == 0), s1_ref[...], prev1)
    prev2 = jnp.where(sample & (r < 2), s2_ref[...], prev2)
    tail = cubuf[tm:tm + SUBLANES, :]
    cubuf[0:SUBLANES, :] = tail
    tail_ref[...] = tail

    @pl.when(sample)
    def _():
        cus_ref[...] = cu

    cw = cw_ref[...]
    y = cw[0:1, :] * prev2 + cw[1:2, :] * prev1 + cw[2:3, :] * cu
    h = x + _dot((b_gate * y).astype(BF16), wout_ref[...])
    h_ref[...] = h
    slab_ref[...] = _route_tail(h, nf_ref[...], wr_ref, brt_ref[...], carry_ref, n_groups, epg)
    cnt_ref[...] = carry_ref[...]


def _mixer_call(xp, xs, s1, s2, seq_len, s_len, na, win, cw, wout, nf, wr, brt, n_groups, epg):
    np_, d = xp.shape
    ns = xs.shape[0]
    tm = TOKEN_TILE
    p_tiles, s_tiles = np_ // tm, ns // tm
    n_tiles = p_tiles + s_tiles
    n = np_ + ns
    p_spec = pl.BlockSpec((tm, d), lambda i: (jnp.minimum(i, p_tiles - 1), 0))
    s_spec = pl.BlockSpec((tm, d), lambda i: (jnp.maximum(i - p_tiles, 0), 0))
    return pl.pallas_call(
        functools.partial(_mixer_kernel, tm=tm, d=d, p_tiles=p_tiles, tiles_per_seq=seq_len // tm, s_len=s_len,
                          n_groups=n_groups, epg=epg),
        grid=(n_tiles,),
        in_specs=[p_spec, s_spec, s_spec, s_spec, _const_spec(na.shape), _const_spec(win.shape),
                  _const_spec(cw.shape), _const_spec(wout.shape), _const_spec(nf.shape), _const_spec(wr.shape),
                  _const_spec(brt.shape)],
        out_specs=[pl.BlockSpec((tm, d), lambda i: (i, 0)), pl.BlockSpec((tm, LANES), lambda i: (i, 0)),
                   pl.BlockSpec((SUBLANES, d), lambda i: (i, 0)), s_spec, _const_spec((1, LANES))],
        out_shape=[jax.ShapeDtypeStruct((n, d), F32), jax.ShapeDtypeStruct((n, LANES), F32),
                   jax.ShapeDtypeStruct((n_tiles * SUBLANES, d), F32), jax.ShapeDtypeStruct((ns, d), F32),
                   jax.ShapeDtypeStruct((1, LANES), F32)],
        scratch_shapes=[pltpu.VMEM((tm + SUBLANES, d), F32), pltpu.VMEM((1, LANES), F32)],
        compiler_params=_cparams(1),
        name="mixer_a",
    )(xp, xs, s1, s2, na, win, cw, wout, nf, wr, brt)


def _invert_kernel(pos_ref, inv_ref, *, n_pairs, n_slots):
    def clear(s, c):
        inv_ref[s] = -1
        return c

    lax.fori_loop(0, n_slots, clear, 0, unroll=DMA_UNROLL)

    def fill(p, c):
        inv_ref[pos_ref[p]] = p
        return c

    lax.fori_loop(0, n_pairs, fill, 0, unroll=DMA_UNROLL)


def _invert_call(pos, n_slots):
    n_pairs = pos.shape[0]
    grid_spec = pltpu.PrefetchScalarGridSpec(
        num_scalar_prefetch=1, grid=(1,), in_specs=[],
        out_specs=pl.BlockSpec(memory_space=pltpu.SMEM))
    return pl.pallas_call(
        functools.partial(_invert_kernel, n_pairs=n_pairs, n_slots=n_slots),
        grid_spec=grid_spec,
        out_shape=jax.ShapeDtypeStruct((n_slots,), I32),
        compiler_params=_cparams(1),
        name="invert_perm",
    )(pos)


def _experts_kernel(te_ref, nv_ref, src_ref, dst_ref, h_hbm, nf_ref, wg_ref, wu_ref, wd_ref, y_hbm,
                    xbuf, ybuf, wg_bf, wu_bf, wd_bf, gsem, ssem, *, tr, n_pairs):
    i = pl.program_id(0)
    nv = nv_ref[0]
    slot = i % 2

    def gather_row(tile, r, s):
        return pltpu.make_async_copy(h_hbm.at[pl.ds(src_ref[tile * tr + r], 1)], xbuf.at[s, pl.ds(r, 1)],
                                     gsem.at[s])

    def scatter_row(tile, r, s):
        return pltpu.make_async_copy(ybuf.at[s, pl.ds(r, 1)], y_hbm.at[pl.ds(dst_ref[tile * tr + r], 1)],
                                     ssem.at[s])

    def start_all(make, tile, s):
        def issue(r, c):
            make(tile, r, s).start()
            return c
        lax.fori_loop(0, tr, issue, 0, unroll=DMA_UNROLL)

    def wait_gather(s):
        pltpu.make_async_copy(h_hbm.at[pl.ds(0, tr)], xbuf.at[s], gsem.at[s]).wait()

    def wait_scatter(s):
        pltpu.make_async_copy(ybuf.at[s], y_hbm.at[pl.ds(0, tr)], ssem.at[s]).wait()

    @pl.when(i < nv)
    def _():
        @pl.when(i == 0)
        def _():
            ybuf[...] = jnp.zeros(ybuf.shape, F32)
            for s in range(2):
                pad = pltpu.make_async_copy(ybuf.at[s], y_hbm.at[pl.ds(n_pairs + s * tr, tr)], ssem.at[s])
                pad.start()
                pad.wait()
            start_all(gather_row, 0, 0)

        start_all(gather_row, i + 1, 1 - slot)
        wait_gather(slot)

        prev = te_ref[jnp.maximum(i - 1, 0)]

        @pl.when((i == 0) | (te_ref[i] != prev))
        def _():
            wg_bf[...] = wg_ref[0].astype(BF16)
            wu_bf[...] = wu_ref[0].astype(BF16)
            wd_bf[...] = wd_ref[0].astype(BF16)

        x = _rms(xbuf[slot], nf_ref[...]).astype(BF16)
        g = _dot(x, wg_bf[...])
        u = _dot(x, wu_bf[...])
        hid = (g * (1.0 / (1.0 + jnp.exp(-g))) * u).astype(BF16)
        y = _dot(hid, wd_bf[...])

        @pl.when(i >= 2)
        def _():
            wait_scatter(slot)

        ybuf[slot] = y
        start_all(scatter_row, i, slot)

        @pl.when(i == nv - 1)
        def _():
            wait_gather(1 - slot)

            @pl.when(i >= 1)
            def _():
                wait_scatter(1 - slot)

            wait_scatter(slot)


def _experts_call(tile_expert, n_valid, src, dst, h, nf, wg, wu, wd, n_tiles):
    n, d = h.shape
    f = wg.shape[-1]
    tr = ROUTE_TILE
    n_pairs = 2 * n
    w_spec = lambda shape: pl.BlockSpec(shape, lambda i, te, nv, s, t: (te[i], 0, 0))
    grid_spec = pltpu.PrefetchScalarGridSpec(
        num_scalar_prefetch=4,
        grid=(n_tiles,),
        in_specs=[pl.BlockSpec(memory_space=pl.ANY),
                  pl.BlockSpec((1, d), lambda i, te, nv, s, t: (0, 0)),
                  w_spec((1, d, f)), w_spec((1, d, f)), w_spec((1, f, d))],
        out_specs=pl.BlockSpec(memory_space=pl.ANY),
        scratch_shapes=[pltpu.VMEM((2, tr, d), F32), pltpu.VMEM((2, tr, d), F32),
                        pltpu.VMEM((d, f), BF16), pltpu.VMEM((d, f), BF16), pltpu.VMEM((f, d), BF16),
                        pltpu.SemaphoreType.DMA((2,)), pltpu.SemaphoreType.DMA((2,))],
    )
    return pl.pallas_call(
        functools.partial(_experts_kernel, tr=tr, n_pairs=n_pairs),
        grid_spec=grid_spec,
        out_shape=jax.ShapeDtypeStruct((n_pairs + 2 * tr, d), F32),
        compiler_params=_cparams(1),
        name="moe_experts",
    )(tile_expert, n_valid, src, dst, h, nf, wg, wu, wd)


def _combine(h, slab, ya, yb):
    return h + (slab[:, 2:3] * ya + slab[:, 3:4] * yb)


def _kvq_kernel(h_ref, slab_ref, ya_ref, yb_ref, nkv_ref, wk_ref, wv_ref, wf_ref, bf_ref, nb_ref, wq_ref,
                h2_ref, k_ref, v_ref, kb_ref, vb_ref, lf_ref, q_ref, *, p_tiles, qs_prompt, qs_sample):
    h2 = _combine(h_ref[...], slab_ref[...], ya_ref[...], yb_ref[...])
    h2_ref[...] = h2
    s = _rms(h2, nkv_ref[...]).astype(BF16)
    k = _dot(s, wk_ref[...])
    v = _dot(s, wv_ref[...])
    k_ref[...] = k
    v_ref[...] = v
    kb_ref[...] = k.astype(BF16)
    vb_ref[...] = v.astype(BF16)
    z = _dot(s, wf_ref[...]) + bf_ref[...]
    lf_ref[...] = -(jnp.maximum(-z, 0.0) + jnp.log1p(jnp.exp(-jnp.abs(z))))
    qn = _rms(h2, nb_ref[...]).astype(BF16)
    q_scale = jnp.where(pl.program_id(0) < p_tiles, qs_prompt, qs_sample)
    q_ref[...] = (_dot(qn, wq_ref[...]) * q_scale).astype(BF16)


def _kvq_call(h, slab, y2, nkv, wk, wv, wf, bf, nb, wq, p_tiles, qs_prompt, qs_sample):
    n, d = h.shape
    hd = wk.shape[1]
    tm = TOKEN_TILE
    nt = n // tm
    tok = lambda w: pl.BlockSpec((tm, w), lambda i: (i, 0))
    return pl.pallas_call(
        functools.partial(_kvq_kernel, p_tiles=p_tiles, qs_prompt=qs_prompt, qs_sample=qs_sample),
        grid=(nt,),
        in_specs=[tok(d), tok(LANES), tok(d), pl.BlockSpec((tm, d), lambda i: (i + nt, 0)),
                  _const_spec(nkv.shape), _const_spec(wk.shape), _const_spec(wv.shape), _const_spec(wf.shape),
                  _const_spec(bf.shape), _const_spec(nb.shape), _const_spec(wq.shape)],
        out_specs=[tok(d), tok(hd), tok(hd), tok(hd), tok(hd), tok(LANES), tok(hd)],
        out_shape=[jax.ShapeDtypeStruct((n, d), F32), jax.ShapeDtypeStruct((n, hd), F32),
                   jax.ShapeDtypeStruct((n, hd), F32), jax.ShapeDtypeStruct((n, hd), BF16),
                   jax.ShapeDtypeStruct((n, hd), BF16), jax.ShapeDtypeStruct((n, LANES), F32),
                   jax.ShapeDtypeStruct((n, hd), BF16)],
        compiler_params=_cparams(1),
        name="combine_kvq",
    )(h, slab, y2, y2, nkv, wk, wv, wf, bf, nb, wq)


def _final_kernel(h_ref, slab_ref, ya_ref, yb_ref, nfin_ref, y_ref):
    y_ref[...] = _rms(_combine(h_ref[...], slab_ref[...], ya_ref[...], yb_ref[...]), nfin_ref[...])


def _final_call(h, slab, y2, nfin):
    n, d = h.shape
    tm = TOKEN_TILE
    nt = n // tm
    tok = lambda w: pl.BlockSpec((tm, w), lambda i: (i, 0))
    return pl.pallas_call(
        _final_kernel,
        grid=(nt,),
        in_specs=[tok(d), tok(LANES), tok(d), pl.BlockSpec((tm, d), lambda i: (i + nt, 0)), _const_spec(nfin.shape)],
        out_specs=tok(d),
        out_shape=jax.ShapeDtypeStruct((n, d), F32),
        compiler_params=_cparams(1),
        name="combine_final",
    )(h, slab, y2, y2, nfin)


def _cumsum_kernel(x_ref, c_ref, *, t):
    rb = x_ref.shape[0]
    row = lax.broadcasted_iota(I32, (LANES, LANES), 0)
    col = lax.broadcasted_iota(I32, (LANES, LANES), 1)
    tri = jnp.where(row <= col, 1.0, 0.0).astype(BF16)
    carry = jnp.zeros((rb, 1), F32)
    for j in range(t // LANES):
        blk = x_ref[:, j * LANES:(j + 1) * LANES]
        a1 = blk.astype(BF16)
        r1 = blk - a1.astype(F32)
        a2 = r1.astype(BF16)
        a3 = (r1 - a2.astype(F32)).astype(BF16)
        cs = _dot(a1, tri) + _dot(a2, tri) + _dot(a3, tri) + carry
        c_ref[:, j * LANES:(j + 1) * LANES] = cs
        carry = cs[:, LANES - 1:LANES]


def _cumsum_call(x):
    r, t = x.shape
    rb = min(r, 128)
    return pl.pallas_call(
        functools.partial(_cumsum_kernel, t=t),
        grid=(r // rb,),
        in_specs=[pl.BlockSpec((rb, t), lambda i: (i, 0))],
        out_specs=pl.BlockSpec((rb, t), lambda i: (i, 0)),
        out_shape=jax.ShapeDtypeStruct((r, t), F32),
        compiler_params=_cparams(1),
        name="logf_cumsum",
    )(x)


def _bias_columns(c2, base, lane, query_side):
    p1 = c2.astype(BF16).astype(F32)
    r1 = c2 - p1
    p2 = r1.astype(BF16).astype(F32)
    p3 = r1 - p2
    off = lane - base
    if query_side:
        ones = (off >= 3) & (off < 6)
        return jnp.where(off == 0, p1, jnp.where(off == 1, p2, jnp.where(off == 2, p3, jnp.where(ones, 1.0, 0.0))))
    ones = (off >= 0) & (off < 3)
    return jnp.where(off == 3, -p1, jnp.where(off == 4, -p2, jnp.where(off == 5, -p3, jnp.where(ones, 1.0, 0.0))))


def _attn_prompt_kernel(q_ref, k_ref, v_ref, c_ref, o_ref, kx0, kx1, vx0, vx1, *, blk, dh, nkv):
    hp = pl.program_id(1)
    qi = pl.program_id(2)
    kx = (kx0, kx1)
    vx = (vx0, vx1)
    lane = lax.broadcasted_iota(I32, (blk, 2 * dh), 1)
    in_head = [(lane >= hd_ * dh) & (lane < (hd_ + 1) * dh) for hd_ in range(2)]
    ext_base = [dh, 0]

    def column(cs, h):
        return jnp.sum(jnp.where(lane == h, cs, 0.0), axis=-1, keepdims=True) * LOG2E

    @pl.when(qi == 0)
    def _():
        for jc in range(nkv):
            rows = slice(jc * blk, (jc + 1) * blk)
            kb = k_ref[rows, :]
            vb = v_ref[rows, :]
            cs = c_ref[rows, :]
            for hd_ in range(2):
                ext = _bias_columns(column(cs, 2 * hp + hd_), ext_base[hd_], lane, False)
                kx[hd_][rows, :] = jnp.where(in_head[hd_], kb, ext.astype(BF16))
                vx[hd_][rows, :] = jnp.where(in_head[hd_], vb, jnp.ones_like(vb))

    q = q_ref[...]
    cs_q = c_ref[pl.ds(pl.multiple_of(qi * blk, blk), blk), :]
    qx = []
    for hd_ in range(2):
        ext = _bias_columns(column(cs_q, 2 * hp + hd_), ext_base[hd_], lane, True)
        qx.append(jnp.where(in_head[hd_], q, ext.astype(BF16)))
    row = lax.broadcasted_iota(I32, (blk, blk), 0)
    col = lax.broadcasted_iota(I32, (blk, blk), 1)
    causal = col <= row

    def scores(j):
        start = pl.multiple_of(j * blk, blk)
        return tuple(_dot_nt(qx[hd_], kx[hd_][pl.ds(start, blk), :]) for hd_ in range(2))

    def accumulate(j, s_pair, carry, masked):
        start = pl.multiple_of(j * blk, blk)
        out = []
        for hd_ in range(2):
            m, acc = carry[hd_]
            s = s_pair[hd_]
            if masked:
                s = jnp.where(causal, s, -jnp.inf)
            m_new = jnp.maximum(m, jnp.max(s, axis=-1, keepdims=True))
            p = jnp.exp2(s - m_new)
            acc = jnp.exp2(m - m_new) * acc + _dot(p.astype(BF16), vx[hd_][pl.ds(start, blk), :])
            out.append((m_new, acc))
        return tuple(out)

    def body(j, state):
        s_cur, carry = state
        s_next = scores(j + 1)
        return s_next, accumulate(j, s_cur, carry, False)

    init = tuple((jnp.full((blk, 1), -jnp.inf, F32), jnp.zeros((blk, 2 * dh), F32)) for _ in range(2))
    s_last, carry = lax.fori_loop(0, qi, body, (scores(0), init))
    (_, acc0), (_, acc1) = accumulate(qi, s_last, carry, True)
    o0 = acc0 / pltpu.roll(acc0, dh, 1)
    o1 = acc1 / pltpu.roll(acc1, dh, 1)
    o_ref[...] = jnp.where(in_head[0], o0, o1).astype(BF16)


def _attn_prompt_call(q, kb, vb, c_slab, batch, seq, n_heads, dh):
    blk = min(ATTN_BLOCK, seq)
    nq = seq // blk
    hd = n_heads * dh
    pair = pl.BlockSpec((seq, 2 * dh), lambda b, hp, qi: (b, hp))
    return pl.pallas_call(
        functools.partial(_attn_prompt_kernel, blk=blk, dh=dh, nkv=nq),
        grid=(batch, n_heads // 2, nq),
        in_specs=[pl.BlockSpec((blk, 2 * dh), lambda b, hp, qi: (b * nq + qi, hp)), pair, pair,
                  pl.BlockSpec((seq, LANES), lambda b, hp, qi: (b, 0))],
        out_specs=pl.BlockSpec((blk, 2 * dh), lambda b, hp, qi: (b * nq + qi, hp)),
        out_shape=jax.ShapeDtypeStruct((batch * seq, hd), BF16),
        scratch_shapes=[pltpu.VMEM((seq, 2 * dh), BF16)] * 4,
        compiler_params=_cparams(3),
        name="attn_prompt",
    )(q, kb, vb, c_slab)


def _attn_sample_kernel(q_ref, kc_ref, vc_ref, kn_ref, vn_ref, cq_ref, ck_ref, o_ref,
                        qbd, m_ref, l_ref, acc_ref, *, ts, n_heads, dh, nchunk):
    j = pl.program_id(1)
    r = n_heads * ts
    hd = n_heads * dh

    @pl.when(j == 0)
    def _():
        q = q_ref[...]
        qt = jnp.concatenate([q] * n_heads, axis=0)
        rr = lax.broadcasted_iota(I32, (r, hd), 0) // ts
        ll = lax.broadcasted_iota(I32, (r, hd), 1) // dh
        qbd[...] = jnp.where(rr == ll, qt, jnp.zeros_like(qt))
        m_ref[...] = jnp.full((r, 1), -jnp.inf, F32)
        l_ref[...] = jnp.zeros((r, 1), F32)
        acc_ref[...] = jnp.zeros((r, hd), F32)

    def expand(ck):
        w = ck.shape[1]
        return jnp.concatenate([jnp.broadcast_to(ck[h:h + 1, :], (ts, w)) for h in range(n_heads)], axis=0)

    def update(s, vb):
        m = m_ref[...]
        m_new = jnp.maximum(m, jnp.max(s, axis=-1, keepdims=True))
        p = jnp.exp(s - m_new)
        alpha = jnp.exp(m - m_new)
        l_ref[...] = alpha * l_ref[...] + jnp.sum(p, axis=-1, keepdims=True)
        acc_ref[...] = alpha * acc_ref[...] + _dot(p.astype(BF16), vb)
        m_ref[...] = m_new

    cq = cq_ref[...][:, 0:1]

    @pl.when(j < nchunk)
    def _():
        kb = kc_ref[...].astype(BF16)
        s = _dot_nt(qbd[...], kb) + (cq - expand(ck_ref[...]))
        update(s, vc_ref[...].astype(BF16))

    @pl.when(j == nchunk)
    def _():
        s = _dot_nt(qbd[...], kn_ref[...]) + (cq - expand(ck_ref[:, 0:ts]))
        qpos = lax.broadcasted_iota(I32, (r, ts), 0) % ts
        kpos = lax.broadcasted_iota(I32, (r, ts), 1)
        s = jnp.where(kpos <= qpos, s, -jnp.inf)
        update(s, vn_ref[...])
        o = acc_ref[...] / l_ref[...]
        ll = lax.broadcasted_iota(I32, (ts, hd), 1) // dh
        out = jnp.zeros((ts, hd), F32)
        for h in range(n_heads):
            out = jnp.where(ll == h, o[h * ts:(h + 1) * ts, :], out)
        o_ref[...] = out.astype(BF16)


def _attn_sample_call(q, cache_k, cache_v, k_new, v_new, cq_rep, ck_chunks, row0, n_streams, ts, n_heads, dh):
    past = cache_k.shape[1]
    hd = n_heads * dh
    ck_len = min(CACHE_CHUNK, past)
    nchunk = past // ck_len
    r = n_heads * ts
    b0 = row0 // ts
    cache_spec = pl.BlockSpec((None, ck_len, hd), lambda b, j: (b, jnp.minimum(j, nchunk - 1), 0))
    new_spec = pl.BlockSpec((ts, hd), lambda b, j: (b0 + b, 0))
    return pl.pallas_call(
        functools.partial(_attn_sample_kernel, ts=ts, n_heads=n_heads, dh=dh, nchunk=nchunk),
        grid=(n_streams, nchunk + 1),
        in_specs=[new_spec, cache_spec, cache_spec, new_spec, new_spec,
                  pl.BlockSpec((r, LANES), lambda b, j: (b, 0)),
                  pl.BlockSpec((n_heads, ck_len), lambda b, j: (b * (nchunk + 1) + j, 0))],
        out_specs=pl.BlockSpec((ts, hd), lambda b, j: (b, 0)),
        out_shape=jax.ShapeDtypeStruct((n_streams * ts, hd), BF16),
        scratch_shapes=[pltpu.VMEM((r, hd), BF16), pltpu.VMEM((r, 1), F32), pltpu.VMEM((r, 1), F32),
                        pltpu.VMEM((r, hd), F32)],
        compiler_params=_cparams(2),
        name="attn_sample",
    )(q, cache_k, cache_v, k_new, v_new, cq_rep, ck_chunks)


def _wo_kernel(h_ref, op_ref, os_ref, wo_ref, nf_ref, wr_ref, brt_ref, h3_ref, slab_ref, cnt_ref, carry_ref,
               *, p_tiles, n_groups, epg):
    i = pl.program_id(0)

    @pl.when(i == 0)
    def _():
        carry_ref[...] = jnp.zeros(carry_ref.shape, F32)

    o = jnp.where(i >= p_tiles, os_ref[...], op_ref[...])
    h3 = h_ref[...] + _dot(o, wo_ref[...])
    h3_ref[...] = h3
    slab_ref[...] = _route_tail(h3, nf_ref[...], wr_ref, brt_ref[...], carry_ref, n_groups, epg)
    cnt_ref[...] = carry_ref[...]


def _wo_call(h, o_p, o_s, wo, nf, wr, brt, n_groups, epg):
    n, d = h.shape
    hd = o_p.shape[1]
    tm = TOKEN_TILE
    p_tiles = o_p.shape[0] // tm
    return pl.pallas_call(
        functools.partial(_wo_kernel, p_tiles=p_tiles, n_groups=n_groups, epg=epg),
        grid=(n // tm,),
        in_specs=[pl.BlockSpec((tm, d), lambda i: (i, 0)),
                  pl.BlockSpec((tm, hd), lambda i: (jnp.minimum(i, p_tiles - 1), 0)),
                  pl.BlockSpec((tm, hd), lambda i: (jnp.maximum(i - p_tiles, 0), 0)),
                  _const_spec(wo.shape), _const_spec(nf.shape), _const_spec(wr.shape), _const_spec(brt.shape)],
        out_specs=[pl.BlockSpec((tm, d), lambda i: (i, 0)), pl.BlockSpec((tm, LANES), lambda i: (i, 0)),
                   _const_spec((1, LANES))],
        out_shape=[jax.ShapeDtypeStruct((n, d), F32), jax.ShapeDtypeStruct((n, LANES), F32),
                   jax.ShapeDtypeStruct((1, LANES), F32)],
        scratch_shapes=[pltpu.VMEM((1, LANES), F32)],
        compiler_params=_cparams(1),
        name="wo_router",
    )(h, o_p, o_s, wo, nf, wr, brt)


def _router_weights(w_group, b_group, w_router, b_router):
    d = w_group.shape[0]
    n = w_group.shape[1] + w_router.shape[1]
    w = jnp.zeros((d, LANES), F32).at[:, :n].set(jnp.concatenate([w_group, w_router], axis=1))
    w1 = w.astype(BF16)
    w2 = (w - w1.astype(F32)).astype(BF16)
    b = jnp.zeros((1, LANES), F32).at[0, :n].set(jnp.concatenate([b_group, b_router]))
    return jnp.stack([w1, w2]), b


def _moe(h, slab, counts, nf, wg, wu, wd, n_groups, n_experts):
    tile = ROUTE_TILE
    n = h.shape[0]
    n_pairs = 2 * n
    cnt = counts[0, n_groups:n_groups + n_experts].astype(I32)
    padded = ((cnt + tile - 1) // tile) * tile
    ends = jnp.cumsum(padded)
    offs = ends - padded
    n_tiles = (n_pairs + n_experts * (tile - 1) + tile - 1) // tile
    n_valid = (ends[-1] // tile).astype(I32)
    starts = jnp.arange(n_tiles, dtype=I32) * tile
    te = jnp.minimum(jnp.sum(starts[:, None] >= ends[None, :], axis=1), n_experts - 1).astype(I32)
    te = jnp.where(jnp.arange(n_tiles) < n_valid, te, te[jnp.maximum(n_valid - 1, 0)])
    e1 = slab[:, 0].astype(I32)
    e2 = slab[:, 1].astype(I32)
    pos = jnp.concatenate([offs[e1] + slab[:, 4].astype(I32), offs[e2] + slab[:, 5].astype(I32)])
    n_slots = (n_tiles + 1) * tile
    inv = _invert_call(pos, n_slots)
    filled = inv >= 0
    src = jnp.where(filled, jnp.where(inv >= n, inv - n, inv), 0)
    dst = jnp.where(filled, inv, n_pairs + jnp.arange(n_slots, dtype=I32) % (2 * tile))
    return _experts_call(te, n_valid.reshape(1), src, dst, h, nf, wg, wu, wd, n_tiles)


def kernel(x_prompt, x_sample, state_conv, cache_k, cache_v, cache_logf, norm_a, w_in_a, conv_w_a, w_out_a,
           norm_kv, w_k, w_v, w_f, b_f, norm_b, w_q_b, w_o_b, norm_ffn, w_group, b_group, w_router, b_router,
           w_gate, w_up, w_down, norm_final):
    bp, tp, d = x_prompt.shape
    bs, ts, _ = x_sample.shape
    past, n_heads, dh = cache_k.shape[1], cache_k.shape[2], cache_k.shape[3]
    hd = n_heads * dh
    n_groups = w_group.shape[-1]
    n_experts = w_gate.shape[1]
    epg = n_experts // n_groups
    np_, ns = bp * tp, bs * ts
    tm = TOKEN_TILE
    assert state_conv.shape[0] == 1 and w_q_b.shape[0] == 1 and state_conv.shape[2] == 2
    assert dh * 2 == LANES and n_heads % 2 == 0
    assert tp % tm == 0 and ns % tm == 0 and tm % ts == 0 and ts >= 2
    p_tiles = np_ // tm

    row = lambda a: a.reshape(1, -1).astype(F32)
    win = w_in_a[0].astype(BF16)
    wout = w_out_a[0].astype(BF16)
    wk, wv, wq, wo = w_k.astype(BF16), w_v.astype(BF16), w_q_b[0].astype(BF16), w_o_b[0].astype(BF16)
    wf = jnp.zeros((d, LANES), F32).at[:, :n_heads].set(w_f).astype(BF16)
    bf = jnp.zeros((1, LANES), F32).at[0, :n_heads].set(b_f)
    wr0, br0 = _router_weights(w_group[0], b_group[0], w_router[0], b_router[0])
    wr1, br1 = _router_weights(w_group[1], b_group[1], w_router[1], b_router[1])

    st = state_conv[0]
    s1 = jnp.zeros((bs, ts, d), F32).at[:, 0].set(st[:, 1]).reshape(ns, d)
    s2 = jnp.zeros((bs, ts, d), F32).at[:, 0].set(st[:, 0]).at[:, 1].set(st[:, 1]).reshape(ns, d)

    h1, slab0, tails, cus, cnt0 = _mixer_call(
        x_prompt.reshape(np_, d), x_sample.reshape(ns, d), s1, s2, tp, ts, row(norm_a[0]), win, conv_w_a[0], wout,
        row(norm_ffn[0]), wr0, br0, n_groups, epg)
    conv_prompt = tails[:p_tiles * SUBLANES].reshape(bp, tp // tm, SUBLANES, d)[:, -1, SUBLANES - 2:][None]
    conv_sample = cus.reshape(bs, ts, d)[:, ts - 2:][None]
    y2 = _moe(h1, slab0, cnt0, row(norm_ffn[0]), w_gate[0], w_up[0], w_down[0], n_groups, n_experts)

    h2, k_all, v_all, kb_all, vb_all, lf_all, q_all = _kvq_call(
        h1, slab0, y2, row(norm_kv), wk, wv, wf, bf, row(norm_b[0]), wq, p_tiles,
        float(dh) ** -0.5 * LOG2E, float(dh) ** -0.5)
    logf_prompt = lf_all[:np_, :n_heads].reshape(bp, tp, n_heads)
    logf_sample = lf_all[np_:, :n_heads].reshape(bs, ts, n_heads)

    c_p = _cumsum_call(logf_prompt.transpose(0, 2, 1).reshape(bp * n_heads, tp))
    c_slab = jnp.zeros((np_, LANES), F32).at[:, :n_heads].set(
        c_p.reshape(bp, n_heads, tp).transpose(0, 2, 1).reshape(np_, n_heads))
    o_p = _attn_prompt_call(q_all, kb_all, vb_all, c_slab, bp, tp, n_heads, dh)

    tall = past + ts
    tpad = ((tall + LANES - 1) // LANES) * LANES
    lfs = jnp.concatenate([cache_logf.astype(F32), logf_sample], axis=1).transpose(0, 2, 1)
    lfs = jnp.pad(lfs, ((0, 0), (0, 0), (0, tpad - tall))).reshape(bs * n_heads, tpad)
    c_s = _cumsum_call(lfs).reshape(bs, n_heads, tpad)
    ck_len = min(CACHE_CHUNK, past)
    nchunk = past // ck_len
    c_past = c_s[:, :, :past].reshape(bs, n_heads, nchunk, ck_len).transpose(0, 2, 1, 3)
    c_new = c_s[:, :, past:past + ts]
    c_new_pad = jnp.pad(c_new, ((0, 0), (0, 0), (0, ck_len - ts)))[:, None]
    ck_s = jnp.concatenate([c_past, c_new_pad], axis=1).reshape(bs * (nchunk + 1) * n_heads, ck_len)
    cq_s = jnp.broadcast_to(c_new.reshape(bs * n_heads * ts, 1), (bs * n_heads * ts, LANES))
    o_s = _attn_sample_call(q_all, cache_k.reshape(bs, past, hd), cache_v.reshape(bs, past, hd), kb_all, vb_all,
                            cq_s, ck_s, np_, bs, ts, n_heads, dh)

    h3, slab1, cnt1 = _wo_call(h2, o_p, o_s, wo, row(norm_ffn[1]), wr1, br1, n_groups, epg)
    y2b = _moe(h3, slab1, cnt1, row(norm_ffn[1]), w_gate[1], w_up[1], w_down[1], n_groups, n_experts)
    y = _final_call(h3, slab1, y2b, row(norm_final))

    k4 = k_all.reshape(np_ + ns, n_heads, dh)
    v4 = v_all.reshape(np_ + ns, n_heads, dh)
    return (y[:np_].reshape(bp, tp, d), y[np_:].reshape(bs, ts, d), conv_prompt, conv_sample,
            k4[:np_].reshape(bp, tp, n_heads, dh), v4[:np_].reshape(bp, tp, n_heads, dh), logf_prompt,
            k4[np_:].reshape(bs, ts, n_heads, dh), v4[np_:].reshape(bs, ts, n_heads, dh), logf_sample)
```

```python
import functools

import jax
import jax.numpy as jnp
from jax import lax
from jax.experimental import pallas as pl
from jax.experimental.pallas import tpu as pltpu

F32 = jnp.float32
BF16 = jnp.bfloat16
I32 = jnp.int32

RMS_EPS = 1e-6
LOG2E = 1.4426950408889634
LANES = 128
SUBLANES = 8
ROUTE_TILE = 256
TOKEN_TILE = 256
ATTN_BLOCK = 512
CACHE_CHUNK = 1024
DMA_UNROLL = 8
VMEM_LIMIT = 56 * 1024 * 1024


def _cparams(n_axes):
    return pltpu.CompilerParams(dimension_semantics=("arbitrary",) * n_axes,
                                vmem_limit_bytes=VMEM_LIMIT)


def _rms(x, g):
    return x * lax.rsqrt(jnp.mean(x * x, axis=-1, keepdims=True) + RMS_EPS) * g


def _dot(a, b):
    return jnp.dot(a, b, preferred_element_type=F32)


def _dot_nt(a, b):
    return lax.dot_general(a, b, (((1,), (1,)), ((), ())), preferred_element_type=F32)


def _const_spec(shape):
    nd = len(shape)
    return pl.BlockSpec(shape, lambda *_: (0,) * nd)


def _route_tail(h, g_ffn, wr_ref, b_rt, carry_ref, n_groups, epg):
    tm = h.shape[0]
    hn = _rms(h, g_ffn)
    a1 = hn.astype(BF16)
    a2 = (hn - a1.astype(F32)).astype(BF16)
    w1 = wr_ref[0]
    w2 = wr_ref[1]
    logits = _dot(a1, w1) + _dot(a1, w2) + _dot(a2, w1) + b_rt
    lane = lax.broadcasted_iota(I32, logits.shape, 1)
    lanef = lane.astype(F32)
    neg = -jnp.inf
    big = 1e9
    gl = jnp.where(lane < n_groups, logits, neg)
    gmax = jnp.max(gl, axis=-1, keepdims=True)
    g_idx = jnp.min(jnp.where(gl == gmax, lanef, big), axis=-1, keepdims=True)
    g_w = 1.0 / jnp.sum(jnp.exp(gl - gmax), axis=-1, keepdims=True)
    lo = n_groups + g_idx * epg
    el = jnp.where((lanef >= lo) & (lanef < lo + epg), logits, neg)
    m1 = jnp.max(el, axis=-1, keepdims=True)
    i1 = jnp.min(jnp.where(el == m1, lanef, big), axis=-1, keepdims=True)
    el2 = jnp.where(lanef == i1, neg, el)
    m2 = jnp.max(el2, axis=-1, keepdims=True)
    i2 = jnp.min(jnp.where(el2 == m2, lanef, big), axis=-1, keepdims=True)
    t = jnp.exp(m2 - m1)
    cw1 = g_w * (1.0 / (1.0 + t))
    cw2 = g_w * (t / (1.0 + t))
    sel1 = lanef == i1
    sel2 = lanef == i2
    oh = jnp.where(sel1 | sel2, 1.0, 0.0)
    row = lax.broadcasted_iota(I32, (tm, tm), 0)
    col = lax.broadcasted_iota(I32, (tm, tm), 1)
    tri = jnp.where(col < row, 1.0, 0.0).astype(BF16)
    carry = carry_ref[...]
    prefix = _dot(tri, oh.astype(BF16)) + carry
    rank1 = jnp.sum(jnp.where(sel1, prefix, 0.0), axis=-1, keepdims=True)
    rank2 = jnp.sum(jnp.where(sel2, prefix, 0.0), axis=-1, keepdims=True)
    carry_ref[...] = carry + jnp.sum(oh, axis=0, keepdims=True)
    slab = jnp.where(lane == 0, i1 - n_groups,
           jnp.where(lane == 1, i2 - n_groups,
           jnp.where(lane == 2, cw1,
           jnp.where(lane == 3, cw2,
           jnp.where(lane == 4, rank1,
           jnp.where(lane == 5, rank2, 0.0))))))
    return slab, jnp.transpose(slab)[0:SUBLANES, :]


def _mixer_kernel(xp_ref, xs_ref, s1_ref, s2_ref, na_ref, win_ref, cw_ref, wout_ref, nf_ref, wr_ref, brt_ref,
                  h_ref, slab_ref, meta_ref, tail_ref, cus_ref, cnt_ref, cubuf, carry_ref,
                  *, tm, d, p_tiles, tiles_per_seq, s_len, n_groups, epg):
    i = pl.program_id(0)
    sample = i >= p_tiles

    @pl.when(i == 0)
    def _():
        carry_ref[...] = jnp.zeros(carry_ref.shape, F32)

    @pl.when(sample | (i % tiles_per_seq == 0))
    def _():
        cubuf[0:SUBLANES, :] = jnp.zeros((SUBLANES, d), F32)

    x = jnp.where(sample, xs_ref[...], xp_ref[...])
    xn = _rms(x, na_ref[...]).astype(BF16)
    bcu = _dot(xn, win_ref[...])
    b_gate = bcu[:, 0:d]
    cu = bcu[:, d:2 * d] * bcu[:, 2 * d:3 * d]
    cubuf[SUBLANES:SUBLANES + tm, :] = cu
    prev1 = cubuf[SUBLANES - 1:SUBLANES - 1 + tm, :]
    prev2 = cubuf[SUBLANES - 2:SUBLANES - 2 + tm, :]
    r = lax.broadcasted_iota(I32, (tm, d), 0) % s_len
    prev1 = jnp.where(sample & (r == 0), s1_ref[...], prev1)
    prev2 = jnp.where(sample & (r < 2), s2_ref[...], prev2)
    tail = cubuf[tm:tm + SUBLANES, :]
    cubuf[0:SUBLANES, :] = tail
    tail_ref[...] = tail

    @pl.when(sample)
    def _():
        cus_ref[...] = cu

    cw = cw_ref[...]
    y = cw[0:1, :] * prev2 + cw[1:2, :] * prev1 + cw[2:3, :] * cu
    h = x + _dot((b_gate * y).astype(BF16), wout_ref[...])
    h_ref[...] = h
    slab_ref[...], meta_ref[...] = _route_tail(h, nf_ref[...], wr_ref, brt_ref[...], carry_ref, n_groups, epg)
    cnt_ref[...] = carry_ref[...]


def _mixer_call(xp, xs, s1, s2, seq_len, s_len, na, win, cw, wout, nf, wr, brt, n_groups, epg):
    np_, d = xp.shape
    ns = xs.shape[0]
    tm = TOKEN_TILE
    p_tiles, s_tiles = np_ // tm, ns // tm
    n_tiles = p_tiles + s_tiles
    n = np_ + ns
    p_spec = pl.BlockSpec((tm, d), lambda i: (jnp.minimum(i, p_tiles - 1), 0))
    s_spec = pl.BlockSpec((tm, d), lambda i: (jnp.maximum(i - p_tiles, 0), 0))
    return pl.pallas_call(
        functools.partial(_mixer_kernel, tm=tm, d=d, p_tiles=p_tiles, tiles_per_seq=seq_len // tm, s_len=s_len,
                          n_groups=n_groups, epg=epg),
        grid=(n_tiles,),
        in_specs=[p_spec, s_spec, s_spec, s_spec, _const_spec(na.shape), _const_spec(win.shape),
                  _const_spec(cw.shape), _const_spec(wout.shape), _const_spec(nf.shape), _const_spec(wr.shape),
                  _const_spec(brt.shape)],
        out_specs=[pl.BlockSpec((tm, d), lambda i: (i, 0)), pl.BlockSpec((tm, LANES), lambda i: (i, 0)),
                   pl.BlockSpec((SUBLANES, tm), lambda i: (i, 0)),
                   pl.BlockSpec((SUBLANES, d), lambda i: (i, 0)), s_spec, _const_spec((1, LANES))],
        out_shape=[jax.ShapeDtypeStruct((n, d), F32), jax.ShapeDtypeStruct((n, LANES), F32),
                   jax.ShapeDtypeStruct((n_tiles * SUBLANES, tm), F32),
                   jax.ShapeDtypeStruct((n_tiles * SUBLANES, d), F32), jax.ShapeDtypeStruct((ns, d), F32),
                   jax.ShapeDtypeStruct((1, LANES), F32)],
        scratch_shapes=[pltpu.VMEM((tm + SUBLANES, d), F32), pltpu.VMEM((1, LANES), F32)],
        compiler_params=_cparams(1),
        name="mixer_a",
    )(xp, xs, s1, s2, na, win, cw, wout, nf, wr, brt)


def _invert_kernel(pos_ref, inv_ref, *, n_pairs, n_slots):
    def clear(s, c):
        inv_ref[s] = -1
        return c

    lax.fori_loop(0, n_slots, clear, 0, unroll=DMA_UNROLL)

    def fill(p, c):
        inv_ref[pos_ref[p]] = p
        return c

    lax.fori_loop(0, n_pairs, fill, 0, unroll=DMA_UNROLL)


def _invert_call(pos, n_slots):
    n_pairs = pos.shape[0]
    grid_spec = pltpu.PrefetchScalarGridSpec(
        num_scalar_prefetch=1, grid=(1,), in_specs=[],
        out_specs=pl.BlockSpec(memory_space=pltpu.SMEM))
    return pl.pallas_call(
        functools.partial(_invert_kernel, n_pairs=n_pairs, n_slots=n_slots),
        grid_spec=grid_spec,
        out_shape=jax.ShapeDtypeStruct((n_slots,), I32),
        compiler_params=_cparams(1),
        name="invert_perm",
    )(pos)


def _experts_kernel(te_ref, nv_ref, src_ref, dst_ref, h_hbm, nf_ref, wg_ref, wu_ref, wd_ref, y_hbm,
                    xbuf, ybuf, wg_bf, wu_bf, wd_bf, gsem, ssem, *, tr, n_pairs):
    i = pl.program_id(0)
    nv = nv_ref[0]

    def gather_rows(tile, s):
        base = tile * tr
        for r in range(tr):
            pltpu.make_async_copy(h_hbm.at[pl.ds(src_ref[base + r], 1)], xbuf.at[s, pl.ds(r, 1)],
                                  gsem.at[s]).start()

    def scatter_rows(dst_tile, s):
        base = dst_tile * tr
        for r in range(tr):
            pltpu.make_async_copy(ybuf.at[s, pl.ds(r, 1)], y_hbm.at[pl.ds(dst_ref[base + r], 1)],
                                  ssem.at[s]).start()

    def wait_gather(s):
        pltpu.make_async_copy(h_hbm.at[pl.ds(0, tr)], xbuf.at[s], gsem.at[s]).wait()

    def wait_scatter(s):
        pltpu.make_async_copy(ybuf.at[s], y_hbm.at[pl.ds(0, tr)], ssem.at[s]).wait()

    def step(s):
        wait_gather(s)
        prev = te_ref[jnp.maximum(i - 1, 0)]

        @pl.when((i == 0) | (te_ref[i] != prev))
        def _():
            wg_bf[...] = wg_ref[0].astype(BF16)
            wu_bf[...] = wu_ref[0].astype(BF16)
            wd_bf[...] = wd_ref[0].astype(BF16)

        x = _rms(xbuf[s], nf_ref[...]).astype(BF16)
        g = _dot(x, wg_bf[...])
        u = _dot(x, wu_bf[...])
        hid = (g * (1.0 / (1.0 + jnp.exp(-g))) * u).astype(BF16)
        y = _dot(hid, wd_bf[...])
        gather_rows(i + 1, 1 - s)
        scatter_rows(i, 1 - s)

        @pl.when(i >= 1)
        def _():
            wait_scatter(s)

        ybuf[s] = y

        @pl.when(i == nv - 1)
        def _():
            scatter_rows(i + 1, s)
            wait_gather(1 - s)
            wait_scatter(1 - s)
            wait_scatter(s)

    @pl.when(i < nv)
    def _():
        @pl.when(i == 0)
        def _():
            ybuf[...] = jnp.zeros(ybuf.shape, F32)
            for s in range(3):
                pad = pltpu.make_async_copy(ybuf.at[0], y_hbm.at[pl.ds(n_pairs + s * tr, tr)], ssem.at[0])
                pad.start()
                pad.wait()
            gather_rows(0, 0)

        @pl.when(i % 2 == 0)
        def _():
            step(0)

        @pl.when(i % 2 == 1)
        def _():
            step(1)


def _experts_call(tile_expert, n_valid, src, dst, h, nf, wg, wu, wd, n_tiles):
    n, d = h.shape
    f = wg.shape[-1]
    tr = ROUTE_TILE
    n_pairs = 2 * n
    w_spec = lambda shape: pl.BlockSpec(shape, lambda i, te, nv, s, t: (te[i], 0, 0))
    grid_spec = pltpu.PrefetchScalarGridSpec(
        num_scalar_prefetch=4,
        grid=(n_tiles,),
        in_specs=[pl.BlockSpec(memory_space=pl.ANY),
                  pl.BlockSpec((1, d), lambda i, te, nv, s, t: (0, 0)),
                  w_spec((1, d, f)), w_spec((1, d, f)), w_spec((1, f, d))],
        out_specs=pl.BlockSpec(memory_space=pl.ANY),
        scratch_shapes=[pltpu.VMEM((2, tr, d), F32), pltpu.VMEM((2, tr, d), F32),
                        pltpu.VMEM((d, f), BF16), pltpu.VMEM((d, f), BF16), pltpu.VMEM((f, d), BF16),
                        pltpu.SemaphoreType.DMA((2,)), pltpu.SemaphoreType.DMA((2,))],
    )
    return pl.pallas_call(
        functools.partial(_experts_kernel, tr=tr, n_pairs=n_pairs),
        grid_spec=grid_spec,
        out_shape=jax.ShapeDtypeStruct((n_pairs + 3 * tr, d), F32),
        compiler_params=_cparams(1),
        name="moe_experts",
    )(tile_expert, n_valid, src, dst, h, nf, wg, wu, wd)


def _combine(h, slab, ya, yb):
    return h + (slab[:, 2:3] * ya + slab[:, 3:4] * yb)


def _kvq_kernel(h_ref, slab_ref, ya_ref, yb_ref, nkv_ref, wk_ref, wv_ref, wf_ref, bf_ref, nb_ref, wq_ref,
                h2_ref, kp_ref, vp_ref, ks_ref, vs_ref, kb_ref, vb_ref, lf_ref, q_ref,
                *, p_tiles, qs_prompt, qs_sample):
    prompt = pl.program_id(0) < p_tiles
    h2 = _combine(h_ref[...], slab_ref[...], ya_ref[...], yb_ref[...])
    h2_ref[...] = h2
    s = _rms(h2, nkv_ref[...]).astype(BF16)
    k = _dot(s, wk_ref[...])
    v = _dot(s, wv_ref[...])

    @pl.when(prompt)
    def _():
        kp_ref[...] = k
        vp_ref[...] = v

    @pl.when(jnp.logical_not(prompt))
    def _():
        ks_ref[...] = k
        vs_ref[...] = v

    kb_ref[...] = k.astype(BF16)
    vb_ref[...] = v.astype(BF16)
    z = _dot(s, wf_ref[...]) + bf_ref[...]
    lf_ref[...] = -(jnp.maximum(-z, 0.0) + jnp.log1p(jnp.exp(-jnp.abs(z))))
    qn = _rms(h2, nb_ref[...]).astype(BF16)
    q_scale = jnp.where(prompt, qs_prompt, qs_sample)
    q_ref[...] = (_dot(qn, wq_ref[...]) * q_scale).astype(BF16)


def _kvq_call(h, slab, y2, nkv, wk, wv, wf, bf, nb, wq, p_tiles, qs_prompt, qs_sample):
    n, d = h.shape
    hd = wk.shape[1]
    tm = TOKEN_TILE
    nt = n // tm
    np_ = p_tiles * tm
    tok = lambda w: pl.BlockSpec((tm, w), lambda i: (i, 0))
    p_out = pl.BlockSpec((tm, hd), lambda i: (jnp.minimum(i, p_tiles - 1), 0))
    s_out = pl.BlockSpec((tm, hd), lambda i: (jnp.maximum(i - p_tiles, 0), 0))
    return pl.pallas_call(
        functools.partial(_kvq_kernel, p_tiles=p_tiles, qs_prompt=qs_prompt, qs_sample=qs_sample),
        grid=(nt,),
        in_specs=[tok(d), tok(LANES), tok(d), pl.BlockSpec((tm, d), lambda i: (i + nt, 0)),
                  _const_spec(nkv.shape), _const_spec(wk.shape), _const_spec(wv.shape), _const_spec(wf.shape),
                  _const_spec(bf.shape), _const_spec(nb.shape), _const_spec(wq.shape)],
        out_specs=[tok(d), p_out, p_out, s_out, s_out, tok(hd), tok(hd), tok(LANES), tok(hd)],
        out_shape=[jax.ShapeDtypeStruct((n, d), F32), jax.ShapeDtypeStruct((np_, hd), F32),
                   jax.ShapeDtypeStruct((np_, hd), F32), jax.ShapeDtypeStruct((n - np_, hd), F32),
                   jax.ShapeDtypeStruct((n - np_, hd), F32), jax.ShapeDtypeStruct((n, hd), BF16),
                   jax.ShapeDtypeStruct((n, hd), BF16), jax.ShapeDtypeStruct((n, LANES), F32),
                   jax.ShapeDtypeStruct((n, hd), BF16)],
        compiler_params=_cparams(1),
        name="combine_kvq",
    )(h, slab, y2, y2, nkv, wk, wv, wf, bf, nb, wq)


def _final_kernel(h_ref, slab_ref, ya_ref, yb_ref, nfin_ref, yp_ref, ys_ref, *, p_tiles):
    y = _rms(_combine(h_ref[...], slab_ref[...], ya_ref[...], yb_ref[...]), nfin_ref[...])
    prompt = pl.program_id(0) < p_tiles

    @pl.when(prompt)
    def _():
        yp_ref[...] = y

    @pl.when(jnp.logical_not(prompt))
    def _():
        ys_ref[...] = y


def _final_call(h, slab, y2, nfin, p_tiles):
    n, d = h.shape
    tm = TOKEN_TILE
    nt = n // tm
    np_ = p_tiles * tm
    tok = lambda w: pl.BlockSpec((tm, w), lambda i: (i, 0))
    return pl.pallas_call(
        functools.partial(_final_kernel, p_tiles=p_tiles),
        grid=(nt,),
        in_specs=[tok(d), tok(LANES), tok(d), pl.BlockSpec((tm, d), lambda i: (i + nt, 0)), _const_spec(nfin.shape)],
        out_specs=[pl.BlockSpec((tm, d), lambda i: (jnp.minimum(i, p_tiles - 1), 0)),
                   pl.BlockSpec((tm, d), lambda i: (jnp.maximum(i - p_tiles, 0), 0))],
        out_shape=[jax.ShapeDtypeStruct((np_, d), F32), jax.ShapeDtypeStruct((n - np_, d), F32)],
        compiler_params=_cparams(1),
        name="combine_final",
    )(h, slab, y2, y2, nfin)


def _cumsum_kernel(x_ref, c_ref, *, t):
    rb = x_ref.shape[0]
    row = lax.broadcasted_iota(I32, (LANES, LANES), 0)
    col = lax.broadcasted_iota(I32, (LANES, LANES), 1)
    tri = jnp.where(row <= col, 1.0, 0.0).astype(BF16)
    carry = jnp.zeros((rb, 1), F32)
    for j in range(t // LANES):
        blk = x_ref[:, j * LANES:(j + 1) * LANES]
        a1 = blk.astype(BF16)
        r1 = blk - a1.astype(F32)
        a2 = r1.astype(BF16)
        a3 = (r1 - a2.astype(F32)).astype(BF16)
        cs = _dot(a1, tri) + _dot(a2, tri) + _dot(a3, tri) + carry
        c_ref[:, j * LANES:(j + 1) * LANES] = cs
        carry = cs[:, LANES - 1:LANES]


def _cumsum_call(x):
    r, t = x.shape
    rb = min(r, 128)
    return pl.pallas_call(
        functools.partial(_cumsum_kernel, t=t),
        grid=(r // rb,),
        in_specs=[pl.BlockSpec((rb, t), lambda i: (i, 0))],
        out_specs=pl.BlockSpec((rb, t), lambda i: (i, 0)),
        out_shape=jax.ShapeDtypeStruct((r, t), F32),
        compiler_params=_cparams(1),
        name="logf_cumsum",
    )(x)


def _bias_columns(c2, base, lane, query_side):
    p1 = c2.astype(BF16).astype(F32)
    r1 = c2 - p1
    p2 = r1.astype(BF16).astype(F32)
    p3 = r1 - p2
    off = lane - base
    if query_side:
        ones = (off >= 3) & (off < 6)
        return jnp.where(off == 0, p1, jnp.where(off == 1, p2, jnp.where(off == 2, p3, jnp.where(ones, 1.0, 0.0))))
    ones = (off >= 0) & (off < 3)
    return jnp.where(off == 3, -p1, jnp.where(off == 4, -p2, jnp.where(off == 5, -p3, jnp.where(ones, 1.0, 0.0))))


def _attn_prompt_kernel(q_ref, k_ref, v_ref, c_ref, o_ref, kx0, kx1, vx0, vx1, *, blk, dh, nkv):
    hp = pl.program_id(1)
    qi = pl.program_id(2)
    kx = (kx0, kx1)
    vx = (vx0, vx1)
    lane = lax.broadcasted_iota(I32, (blk, 2 * dh), 1)
    in_head = [(lane >= hd_ * dh) & (lane < (hd_ + 1) * dh) for hd_ in range(2)]
    ext_base = [dh, 0]

    def column(cs, h):
        return jnp.sum(jnp.where(lane == h, cs, 0.0), axis=-1, keepdims=True) * LOG2E

    @pl.when(qi == 0)
    def _():
        for jc in range(nkv):
            rows = slice(jc * blk, (jc + 1) * blk)
            kb = k_ref[rows, :]
            vb = v_ref[rows, :]
            cs = c_ref[rows, :]
            for hd_ in range(2):
                ext = _bias_columns(column(cs, 2 * hp + hd_), ext_base[hd_], lane, False)
                kx[hd_][rows, :] = jnp.where(in_head[hd_], kb, ext.astype(BF16))
                vx[hd_][rows, :] = jnp.where(in_head[hd_], vb, jnp.ones_like(vb))

    q = q_ref[...]
    cs_q = c_ref[pl.ds(pl.multiple_of(qi * blk, blk), blk), :]
    qx = []
    for hd_ in range(2):
        ext = _bias_columns(column(cs_q, 2 * hp + hd_), ext_base[hd_], lane, True)
        qx.append(jnp.where(in_head[hd_], q, ext.astype(BF16)))
    row = lax.broadcasted_iota(I32, (blk, blk), 0)
    col = lax.broadcasted_iota(I32, (blk, blk), 1)
    causal = col <= row

    def scores(j):
        start = pl.multiple_of(j * blk, blk)
        return tuple(_dot_nt(qx[hd_], kx[hd_][pl.ds(start, blk), :]) for hd_ in range(2))

    def accumulate(j, s_pair, carry, masked):
        start = pl.multiple_of(j * blk, blk)
        out = []
        for hd_ in range(2):
            m, acc = carry[hd_]
            s = s_pair[hd_]
            if masked:
                s = jnp.where(causal, s, -jnp.inf)
            m_new = jnp.maximum(m, jnp.max(s, axis=-1, keepdims=True))
            p = jnp.exp2(s - m_new)
            acc = jnp.exp2(m - m_new) * acc + _dot(p.astype(BF16), vx[hd_][pl.ds(start, blk), :])
            out.append((m_new, acc))
        return tuple(out)

    def body(j, state):
        s_cur, carry = state
        s_next = scores(j + 1)
        return s_next, accumulate(j, s_cur, carry, False)

    init = tuple((jnp.full((blk, 1), -jnp.inf, F32), jnp.zeros((blk, 2 * dh), F32)) for _ in range(2))
    s_last, carry = lax.fori_loop(0, qi, body, (scores(0), init))
    (_, acc0), (_, acc1) = accumulate(qi, s_last, carry, True)
    o0 = acc0 / pltpu.roll(acc0, dh, 1)
    o1 = acc1 / pltpu.roll(acc1, dh, 1)
    o_ref[...] = jnp.where(in_head[0], o0, o1).astype(BF16)


def _attn_prompt_call(q, kb, vb, c_slab, batch, seq, n_heads, dh):
    blk = min(ATTN_BLOCK, seq)
    nq = seq // blk
    hd = n_heads * dh
    pair = pl.BlockSpec((seq, 2 * dh), lambda b, hp, qi: (b, hp))
    return pl.pallas_call(
        functools.partial(_attn_prompt_kernel, blk=blk, dh=dh, nkv=nq),
        grid=(batch, n_heads // 2, nq),
        in_specs=[pl.BlockSpec((blk, 2 * dh), lambda b, hp, qi: (b * nq + qi, hp)), pair, pair,
                  pl.BlockSpec((seq, LANES), lambda b, hp, qi: (b, 0))],
        out_specs=pl.BlockSpec((blk, 2 * dh), lambda b, hp, qi: (b * nq + qi, hp)),
        out_shape=jax.ShapeDtypeStruct((batch * seq, hd), BF16),
        scratch_shapes=[pltpu.VMEM((seq, 2 * dh), BF16)] * 4,
        compiler_params=_cparams(3),
        name="attn_prompt",
    )(q, kb, vb, c_slab)


def _attn_sample_kernel(q_ref, kc_ref, vc_ref, kn_ref, vn_ref, cq_ref, ck_ref, o_ref,
                        qbd, m_ref, l_ref, acc_ref, *, ts, n_heads, dh, nchunk):
    j = pl.program_id(1)
    r = n_heads * ts
    hd = n_heads * dh

    @pl.when(j == 0)
    def _():
        q = q_ref[...]
        qt = jnp.concatenate([q] * n_heads, axis=0)
        rr = lax.broadcasted_iota(I32, (r, hd), 0) // ts
        ll = lax.broadcasted_iota(I32, (r, hd), 1) // dh
        qbd[...] = jnp.where(rr == ll, qt, jnp.zeros_like(qt))
        m_ref[...] = jnp.full((r, 1), -jnp.inf, F32)
        l_ref[...] = jnp.zeros((r, 1), F32)
        acc_ref[...] = jnp.zeros((r, hd), F32)

    def expand(ck):
        w = ck.shape[1]
        return jnp.concatenate([jnp.broadcast_to(ck[h:h + 1, :], (ts, w)) for h in range(n_heads)], axis=0)

    def update(s, vb):
        m = m_ref[...]
        m_new = jnp.maximum(m, jnp.max(s, axis=-1, keepdims=True))
        p = jnp.exp(s - m_new)
        alpha = jnp.exp(m - m_new)
        l_ref[...] = alpha * l_ref[...] + jnp.sum(p, axis=-1, keepdims=True)
        acc_ref[...] = alpha * acc_ref[...] + _dot(p.astype(BF16), vb)
        m_ref[...] = m_new

    cq = cq_ref[...][:, 0:1]

    @pl.when(j < nchunk)
    def _():
        kb = kc_ref[...].astype(BF16)
        s = _dot_nt(qbd[...], kb) + (cq - expand(ck_ref[...]))
        update(s, vc_ref[...].astype(BF16))

    @pl.when(j == nchunk)
    def _():
        s = _dot_nt(qbd[...], kn_ref[...]) + (cq - expand(ck_ref[:, 0:ts]))
        qpos = lax.broadcasted_iota(I32, (r, ts), 0) % ts
        kpos = lax.broadcasted_iota(I32, (r, ts), 1)
        s = jnp.where(kpos <= qpos, s, -jnp.inf)
        update(s, vn_ref[...])
        o = acc_ref[...] / l_ref[...]
        ll = lax.broadcasted_iota(I32, (ts, hd), 1) // dh
        out = jnp.zeros((ts, hd), F32)
        for h in range(n_heads):
            out = jnp.where(ll == h, o[h * ts:(h + 1) * ts, :], out)
        o_ref[...] = out.astype(BF16)


def _attn_sample_call(q, cache_k, cache_v, k_new, v_new, cq_rep, ck_chunks, row0, n_streams, ts, n_heads, dh):
    past = cache_k.shape[1]
    hd = n_heads * dh
    ck_len = min(CACHE_CHUNK, past)
    nchunk = past // ck_len
    r = n_heads * ts
    b0 = row0 // ts
    cache_spec = pl.BlockSpec((None, ck_len, hd), lambda b, j: (b, jnp.minimum(j, nchunk - 1), 0))
    new_spec = pl.BlockSpec((ts, hd), lambda b, j: (b0 + b, 0))
    return pl.pallas_call(
        functools.partial(_attn_sample_kernel, ts=ts, n_heads=n_heads, dh=dh, nchunk=nchunk),
        grid=(n_streams, nchunk + 1),
        in_specs=[new_spec, cache_spec, cache_spec, new_spec, new_spec,
                  pl.BlockSpec((r, LANES), lambda b, j: (b, 0)),
                  pl.BlockSpec((n_heads, ck_len), lambda b, j: (b * (nchunk + 1) + j, 0))],
        out_specs=pl.BlockSpec((ts, hd), lambda b, j: (b, 0)),
        out_shape=jax.ShapeDtypeStruct((n_streams * ts, hd), BF16),
        scratch_shapes=[pltpu.VMEM((r, hd), BF16), pltpu.VMEM((r, 1), F32), pltpu.VMEM((r, 1), F32),
                        pltpu.VMEM((r, hd), F32)],
        compiler_params=_cparams(2),
        name="attn_sample",
    )(q, cache_k, cache_v, k_new, v_new, cq_rep, ck_chunks)


def _wo_kernel(h_ref, op_ref, os_ref, wo_ref, nf_ref, wr_ref, brt_ref, h3_ref, slab_ref, meta_ref, cnt_ref,
               carry_ref, *, p_tiles, n_groups, epg):
    i = pl.program_id(0)

    @pl.when(i == 0)
    def _():
        carry_ref[...] = jnp.zeros(carry_ref.shape, F32)

    o = jnp.where(i >= p_tiles, os_ref[...], op_ref[...])
    h3 = h_ref[...] + _dot(o, wo_ref[...])
    h3_ref[...] = h3
    slab_ref[...], meta_ref[...] = _route_tail(h3, nf_ref[...], wr_ref, brt_ref[...], carry_ref, n_groups, epg)
    cnt_ref[...] = carry_ref[...]


def _wo_call(h, o_p, o_s, wo, nf, wr, brt, n_groups, epg):
    n, d = h.shape
    hd = o_p.shape[1]
    tm = TOKEN_TILE
    p_tiles = o_p.shape[0] // tm
    return pl.pallas_call(
        functools.partial(_wo_kernel, p_tiles=p_tiles, n_groups=n_groups, epg=epg),
        grid=(n // tm,),
        in_specs=[pl.BlockSpec((tm, d), lambda i: (i, 0)),
                  pl.BlockSpec((tm, hd), lambda i: (jnp.minimum(i, p_tiles - 1), 0)),
                  pl.BlockSpec((tm, hd), lambda i: (jnp.maximum(i - p_tiles, 0), 0)),
                  _const_spec(wo.shape), _const_spec(nf.shape), _const_spec(wr.shape), _const_spec(brt.shape)],
        out_specs=[pl.BlockSpec((tm, d), lambda i: (i, 0)), pl.BlockSpec((tm, LANES), lambda i: (i, 0)),
                   pl.BlockSpec((SUBLANES, tm), lambda i: (i, 0)), _const_spec((1, LANES))],
        out_shape=[jax.ShapeDtypeStruct((n, d), F32), jax.ShapeDtypeStruct((n, LANES), F32),
                   jax.ShapeDtypeStruct((n // tm * SUBLANES, tm), F32), jax.ShapeDtypeStruct((1, LANES), F32)],
        scratch_shapes=[pltpu.VMEM((1, LANES), F32)],
        compiler_params=_cparams(1),
        name="wo_router",
    )(h, o_p, o_s, wo, nf, wr, brt)


def _router_weights(w_group, b_group, w_router, b_router):
    d = w_group.shape[0]
    n = w_group.shape[1] + w_router.shape[1]
    w = jnp.zeros((d, LANES), F32).at[:, :n].set(jnp.concatenate([w_group, w_router], axis=1))
    w1 = w.astype(BF16)
    w2 = (w - w1.astype(F32)).astype(BF16)
    b = jnp.zeros((1, LANES), F32).at[0, :n].set(jnp.concatenate([b_group, b_router]))
    return jnp.stack([w1, w2]), b


def _moe(h, meta, counts, nf, wg, wu, wd, n_groups, n_experts):
    tile = ROUTE_TILE
    n = h.shape[0]
    n_pairs = 2 * n
    cnt = counts[0, n_groups:n_groups + n_experts].astype(I32)
    padded = ((cnt + tile - 1) // tile) * tile
    ends = jnp.cumsum(padded)
    offs = ends - padded
    n_tiles = (n_pairs + n_experts * (tile - 1) + tile - 1) // tile
    n_valid = (ends[-1] // tile).astype(I32)
    starts = jnp.arange(n_tiles, dtype=I32) * tile
    te = jnp.zeros((n_tiles,), I32)
    for e in range(n_experts - 1):
        te = te + (starts >= ends[e]).astype(I32)
    te = jnp.where(jnp.arange(n_tiles) < n_valid, te, te[jnp.maximum(n_valid - 1, 0)])
    m = meta.reshape(-1, SUBLANES, meta.shape[1]).astype(I32)
    ids = m[:, 0:2, :].transpose(1, 0, 2).reshape(2, n)
    ranks = m[:, 4:6, :].transpose(1, 0, 2).reshape(2, n)
    base = jnp.zeros((2, n), I32)
    for e in range(n_experts):
        base = jnp.where(ids == e, offs[e], base)
    pos = (base + ranks).reshape(n_pairs)
    n_slots = (n_tiles + 1) * tile
    inv = _invert_call(pos, n_slots)
    filled = inv >= 0
    slot_id = jnp.arange(n_slots, dtype=I32)
    src = jnp.where(filled, jnp.where(inv >= n, inv - n, inv), 0)
    dst = jnp.where(filled, inv, n_pairs + slot_id % (2 * tile))
    dst = jnp.concatenate([n_pairs + 2 * tile + slot_id[:tile], dst[:n_tiles * tile]])
    return _experts_call(te, n_valid.reshape(1), src, dst, h, nf, wg, wu, wd, n_tiles)


def kernel(x_prompt, x_sample, state_conv, cache_k, cache_v, cache_logf, norm_a, w_in_a, conv_w_a, w_out_a,
           norm_kv, w_k, w_v, w_f, b_f, norm_b, w_q_b, w_o_b, norm_ffn, w_group, b_group, w_router, b_router,
           w_gate, w_up, w_down, norm_final):
    bp, tp, d = x_prompt.shape
    bs, ts, _ = x_sample.shape
    past, n_heads, dh = cache_k.shape[1], cache_k.shape[2], cache_k.shape[3]
    hd = n_heads * dh
    n_groups = w_group.shape[-1]
    n_experts = w_gate.shape[1]
    epg = n_experts // n_groups
    np_, ns = bp * tp, bs * ts
    tm = TOKEN_TILE
    assert state_conv.shape[0] == 1 and w_q_b.shape[0] == 1 and state_conv.shape[2] == 2
    assert dh * 2 == LANES and n_heads % 2 == 0
    assert tp % tm == 0 and ns % tm == 0 and tm % ts == 0 and ts >= 2
    p_tiles = np_ // tm

    row = lambda a: a.reshape(1, -1).astype(F32)
    win = w_in_a[0].astype(BF16)
    wout = w_out_a[0].astype(BF16)
    wk, wv, wq, wo = w_k.astype(BF16), w_v.astype(BF16), w_q_b[0].astype(BF16), w_o_b[0].astype(BF16)
    wf = jnp.zeros((d, LANES), F32).at[:, :n_heads].set(w_f).astype(BF16)
    bf = jnp.zeros((1, LANES), F32).at[0, :n_heads].set(b_f)
    wr0, br0 = _router_weights(w_group[0], b_group[0], w_router[0], b_router[0])
    wr1, br1 = _router_weights(w_group[1], b_group[1], w_router[1], b_router[1])

    st = state_conv[0]
    s1 = jnp.zeros((bs, ts, d), F32).at[:, 0].set(st[:, 1]).reshape(ns, d)
    s2 = jnp.zeros((bs, ts, d), F32).at[:, 0].set(st[:, 0]).at[:, 1].set(st[:, 1]).reshape(ns, d)

    h1, slab0, meta0, tails, cus, cnt0 = _mixer_call(
        x_prompt.reshape(np_, d), x_sample.reshape(ns, d), s1, s2, tp, ts, row(norm_a[0]), win, conv_w_a[0], wout,
        row(norm_ffn[0]), wr0, br0, n_groups, epg)
    conv_prompt = tails[:p_tiles * SUBLANES].reshape(bp, tp // tm, SUBLANES, d)[:, -1, SUBLANES - 2:][None]
    conv_sample = cus.reshape(bs, ts, d)[:, ts - 2:][None]
    y2 = _moe(h1, meta0, cnt0, row(norm_ffn[0]), w_gate[0], w_up[0], w_down[0], n_groups, n_experts)

    h2, k_p, v_p, k_s, v_s, kb_all, vb_all, lf_all, q_all = _kvq_call(
        h1, slab0, y2, row(norm_kv), wk, wv, wf, bf, row(norm_b[0]), wq, p_tiles,
        float(dh) ** -0.5 * LOG2E, float(dh) ** -0.5)
    logf_prompt = lf_all[:np_, :n_heads].reshape(bp, tp, n_heads)
    logf_sample = lf_all[np_:, :n_heads].reshape(bs, ts, n_heads)

    c_p = _cumsum_call(logf_prompt.transpose(0, 2, 1).reshape(bp * n_heads, tp))
    c_slab = jnp.zeros((np_, LANES), F32).at[:, :n_heads].set(
        c_p.reshape(bp, n_heads, tp).transpose(0, 2, 1).reshape(np_, n_heads))
    o_p = _attn_prompt_call(q_all, kb_all, vb_all, c_slab, bp, tp, n_heads, dh)

    tall = past + ts
    tpad = ((tall + LANES - 1) // LANES) * LANES
    lfs = jnp.concatenate([cache_logf.astype(F32), logf_sample], axis=1).transpose(0, 2, 1)
    lfs = jnp.pad(lfs, ((0, 0), (0, 0), (0, tpad - tall))).reshape(bs * n_heads, tpad)
    c_s = _cumsum_call(lfs).reshape(bs, n_heads, tpad)
    ck_len = min(CACHE_CHUNK, past)
    nchunk = past // ck_len
    c_past = c_s[:, :, :past].reshape(bs, n_heads, nchunk, ck_len).transpose(0, 2, 1, 3)
    c_new = c_s[:, :, past:past + ts]
    c_new_pad = jnp.pad(c_new, ((0, 0), (0, 0), (0, ck_len - ts)))[:, None]
    ck_s = jnp.concatenate([c_past, c_new_pad], axis=1).reshape(bs * (nchunk + 1) * n_heads, ck_len)
    cq_s = jnp.broadcast_to(c_new.reshape(bs * n_heads * ts, 1), (bs * n_heads * ts, LANES))
    o_s = _attn_sample_call(q_all, cache_k.reshape(bs, past, hd), cache_v.reshape(bs, past, hd), kb_all, vb_all,
                            cq_s, ck_s, np_, bs, ts, n_heads, dh)

    h3, slab1, meta1, cnt1 = _wo_call(h2, o_p, o_s, wo, row(norm_ffn[1]), wr1, br1, n_groups, epg)
    y2b = _moe(h3, meta1, cnt1, row(norm_ffn[1]), w_gate[1], w_up[1], w_down[1], n_groups, n_experts)
    y_p, y_s = _final_call(h3, slab1, y2b, row(norm_final), p_tiles)

    return (y_p.reshape(bp, tp, d), y_s.reshape(bs, ts, d), conv_prompt, conv_sample,
            k_p.reshape(bp, tp, n_heads, dh), v_p.reshape(bp, tp, n_heads, dh), logf_prompt,
            k_s.reshape(bs, ts, n_heads, dh), v_s.reshape(bs, ts, n_heads, dh), logf_sample)
```

```python
import functools

import jax
import jax.numpy as jnp
from jax import lax
from jax.experimental import pallas as pl
from jax.experimental.pallas import tpu as pltpu

F32 = jnp.float32
BF16 = jnp.bfloat16
I32 = jnp.int32

RMS_EPS = 1e-6
LOG2E = 1.4426950408889634
LANES = 128
SUBLANES = 8
ROUTE_TILE = 256
TOKEN_TILE = 512
ATTN_BLOCK = 512
CACHE_CHUNK = 1024
DMA_UNROLL = 8
VMEM_LIMIT = 56 * 1024 * 1024


def _cparams(n_axes):
    return pltpu.CompilerParams(dimension_semantics=("arbitrary",) * n_axes,
                                vmem_limit_bytes=VMEM_LIMIT)


def _rms(x, g):
    return x * lax.rsqrt(jnp.mean(x * x, axis=-1, keepdims=True) + RMS_EPS) * g


def _dot(a, b):
    return jnp.dot(a, b, preferred_element_type=F32)


def _dot_nt(a, b):
    return lax.dot_general(a, b, (((1,), (1,)), ((), ())), preferred_element_type=F32)


def _tok_load(ref, rows, d):
    c = d // LANES
    return jnp.concatenate([ref[pl.ds(j, rows, stride=c), :] for j in range(c)], axis=1)


def _tok_store(ref, val):
    rows, d = val.shape
    c = d // LANES
    for j in range(c):
        ref[pl.ds(j, rows, stride=c), :] = val[:, j * LANES:(j + 1) * LANES]


def _tok_spec(rows, d, index_map):
    return pl.BlockSpec((rows * (d // LANES), LANES), index_map)


def _const_spec(shape):
    nd = len(shape)
    return pl.BlockSpec(shape, lambda *_: (0,) * nd, pipeline_mode=pl.Buffered(1))


def _route_tail(h, g_ffn, wr_ref, b_rt, carry_ref, n_groups, epg):
    tm = h.shape[0]
    hn = _rms(h, g_ffn)
    a1 = hn.astype(BF16)
    a2 = (hn - a1.astype(F32)).astype(BF16)
    w1 = wr_ref[0]
    w2 = wr_ref[1]
    logits = _dot(a1, w1) + _dot(a1, w2) + _dot(a2, w1) + b_rt
    lane = lax.broadcasted_iota(I32, logits.shape, 1)
    lanef = lane.astype(F32)
    neg = -jnp.inf
    big = 1e9
    gl = jnp.where(lane < n_groups, logits, neg)
    gmax = jnp.max(gl, axis=-1, keepdims=True)
    g_idx = jnp.min(jnp.where(gl == gmax, lanef, big), axis=-1, keepdims=True)
    g_w = 1.0 / jnp.sum(jnp.exp(gl - gmax), axis=-1, keepdims=True)
    lo = n_groups + g_idx * epg
    el = jnp.where((lanef >= lo) & (lanef < lo + epg), logits, neg)
    m1 = jnp.max(el, axis=-1, keepdims=True)
    i1 = jnp.min(jnp.where(el == m1, lanef, big), axis=-1, keepdims=True)
    el2 = jnp.where(lanef == i1, neg, el)
    m2 = jnp.max(el2, axis=-1, keepdims=True)
    i2 = jnp.min(jnp.where(el2 == m2, lanef, big), axis=-1, keepdims=True)
    t = jnp.exp(m2 - m1)
    cw1 = g_w * (1.0 / (1.0 + t))
    cw2 = g_w * (t / (1.0 + t))
    sel1 = lanef == i1
    sel2 = lanef == i2
    oh = jnp.where(sel1 | sel2, 1.0, 0.0)
    row = lax.broadcasted_iota(I32, (tm, tm), 0)
    col = lax.broadcasted_iota(I32, (tm, tm), 1)
    tri = jnp.where(col < row, 1.0, 0.0).astype(BF16)
    carry = carry_ref[...]
    prefix = _dot(tri, oh.astype(BF16)) + carry
    rank1 = jnp.sum(jnp.where(sel1, prefix, 0.0), axis=-1, keepdims=True)
    rank2 = jnp.sum(jnp.where(sel2, prefix, 0.0), axis=-1, keepdims=True)
    carry_ref[...] = carry + jnp.sum(oh, axis=0, keepdims=True)
    slab = jnp.where(lane == 0, i1 - n_groups,
           jnp.where(lane == 1, i2 - n_groups,
           jnp.where(lane == 2, cw1,
           jnp.where(lane == 3, cw2,
           jnp.where(lane == 4, rank1,
           jnp.where(lane == 5, rank2, 0.0))))))
    return slab, jnp.transpose(slab)[0:SUBLANES, :]


def _mixer_kernel(xp_ref, xs_ref, s1_ref, s2_ref, na_ref, win_ref, cw_ref, wout_ref, nf_ref, wr_ref, brt_ref,
                  h_ref, slab_ref, meta_ref, tail_ref, cus_ref, cnt_ref, cubuf, carry_ref,
                  *, tm, d, p_tiles, tiles_per_seq, s_len, n_groups, epg):
    i = pl.program_id(0)
    sample = i >= p_tiles

    @pl.when(i == 0)
    def _():
        carry_ref[...] = jnp.zeros(carry_ref.shape, F32)

    @pl.when(sample | (i % tiles_per_seq == 0))
    def _():
        cubuf[0:SUBLANES, :] = jnp.zeros((SUBLANES, d), F32)

    x = jnp.where(sample, xs_ref[...], xp_ref[...])
    xn = _rms(x, na_ref[...]).astype(BF16)
    bcu = _dot(xn, win_ref[...])
    b_gate = bcu[:, 0:d]
    cu = bcu[:, d:2 * d] * bcu[:, 2 * d:3 * d]
    cubuf[SUBLANES:SUBLANES + tm, :] = cu
    prev1 = cubuf[SUBLANES - 1:SUBLANES - 1 + tm, :]
    prev2 = cubuf[SUBLANES - 2:SUBLANES - 2 + tm, :]
    r = lax.broadcasted_iota(I32, (tm, d), 0) % s_len
    prev1 = jnp.where(sample & (r == 0), s1_ref[...], prev1)
    prev2 = jnp.where(sample & (r < 2), s2_ref[...], prev2)
    tail = cubuf[tm:tm + SUBLANES, :]
    cubuf[0:SUBLANES, :] = tail
    tail_ref[...] = tail

    @pl.when(sample)
    def _():
        cus_ref[...] = cu

    cw = cw_ref[...]
    y = cw[0:1, :] * prev2 + cw[1:2, :] * prev1 + cw[2:3, :] * cu
    h = x + _dot((b_gate * y).astype(BF16), wout_ref[...])
    _tok_store(h_ref, h)
    slab_ref[...], meta_ref[...] = _route_tail(h, nf_ref[...], wr_ref, brt_ref[...], carry_ref, n_groups, epg)
    cnt_ref[...] = carry_ref[...]


def _mixer_call(xp, xs, s1, s2, seq_len, s_len, na, win, cw, wout, nf, wr, brt, n_groups, epg):
    np_, d = xp.shape
    ns = xs.shape[0]
    tm = TOKEN_TILE
    p_tiles, s_tiles = np_ // tm, ns // tm
    n_tiles = p_tiles + s_tiles
    n = np_ + ns
    p_spec = pl.BlockSpec((tm, d), lambda i: (jnp.minimum(i, p_tiles - 1), 0))
    s_spec = pl.BlockSpec((tm, d), lambda i: (jnp.maximum(i - p_tiles, 0), 0))
    return pl.pallas_call(
        functools.partial(_mixer_kernel, tm=tm, d=d, p_tiles=p_tiles, tiles_per_seq=seq_len // tm, s_len=s_len,
                          n_groups=n_groups, epg=epg),
        grid=(n_tiles,),
        in_specs=[p_spec, s_spec, s_spec, s_spec, _const_spec(na.shape), _const_spec(win.shape),
                  _const_spec(cw.shape), _const_spec(wout.shape), _const_spec(nf.shape), _const_spec(wr.shape),
                  _const_spec(brt.shape)],
        out_specs=[_tok_spec(tm, d, lambda i: (i, 0)), pl.BlockSpec((tm, LANES), lambda i: (i, 0)),
                   pl.BlockSpec((SUBLANES, tm), lambda i: (i, 0)),
                   pl.BlockSpec((SUBLANES, d), lambda i: (i, 0)), s_spec,
                   pl.BlockSpec((1, LANES), lambda i: (0, 0))],
        out_shape=[jax.ShapeDtypeStruct((n * (d // LANES), LANES), F32), jax.ShapeDtypeStruct((n, LANES), F32),
                   jax.ShapeDtypeStruct((n_tiles * SUBLANES, tm), F32),
                   jax.ShapeDtypeStruct((n_tiles * SUBLANES, d), F32), jax.ShapeDtypeStruct((ns, d), F32),
                   jax.ShapeDtypeStruct((1, LANES), F32)],
        scratch_shapes=[pltpu.VMEM((tm + SUBLANES, d), F32), pltpu.VMEM((1, LANES), F32)],
        compiler_params=_cparams(1),
        name="mixer_a",
    )(xp, xs, s1, s2, na, win, cw, wout, nf, wr, brt)


def _invert_kernel(pos_ref, empty_hbm, inv_ref, *, n_pairs):
    pltpu.sync_copy(empty_hbm, inv_ref)

    def fill(p, c):
        inv_ref[pos_ref[p]] = p
        return c

    lax.fori_loop(0, n_pairs, fill, 0, unroll=DMA_UNROLL)


def _invert_call(pos, n_slots):
    n_pairs = pos.shape[0]
    grid_spec = pltpu.PrefetchScalarGridSpec(
        num_scalar_prefetch=1, grid=(1,), in_specs=[pl.BlockSpec(memory_space=pl.ANY)],
        out_specs=pl.BlockSpec(memory_space=pltpu.SMEM))
    return pl.pallas_call(
        functools.partial(_invert_kernel, n_pairs=n_pairs),
        grid_spec=grid_spec,
        out_shape=jax.ShapeDtypeStruct((n_slots,), I32),
        compiler_params=_cparams(1),
        name="invert_perm",
    )(pos, jnp.full((n_slots,), -1, I32))


def _experts_kernel(te_ref, nv_ref, src_ref, dst_ref, h_hbm, nf_ref, wg_ref, wu_ref, wd_ref, y_hbm,
                    xbuf, ybuf, wg_bf, wu_bf, wd_bf, gsem, ssem, *, tr, d, n_pairs):
    i = pl.program_id(0)
    nv = nv_ref[0]
    c = d // LANES

    def gather_rows(tile, s):
        base = tile * tr
        for r in range(tr):
            row0 = pl.multiple_of(src_ref[base + r], c)
            pltpu.make_async_copy(h_hbm.at[pl.ds(row0, c)], xbuf.at[s, pl.ds(r * c, c)], gsem.at[s]).start()

    def scatter_rows(dst_tile, s):
        base = dst_tile * tr
        for r in range(tr):
            row0 = pl.multiple_of(dst_ref[base + r], c)
            pltpu.make_async_copy(ybuf.at[s, pl.ds(r * c, c)], y_hbm.at[pl.ds(row0, c)], ssem.at[s]).start()

    def wait_gather(s):
        pltpu.make_async_copy(h_hbm.at[pl.ds(0, tr * c)], xbuf.at[s], gsem.at[s]).wait()

    def wait_scatter(s):
        pltpu.make_async_copy(ybuf.at[s], y_hbm.at[pl.ds(0, tr * c)], ssem.at[s]).wait()

    def step(s):
        wait_gather(s)
        prev = te_ref[jnp.maximum(i - 1, 0)]

        @pl.when((i == 0) | (te_ref[i] != prev))
        def _():
            wg_bf[...] = wg_ref[0].astype(BF16)
            wu_bf[...] = wu_ref[0].astype(BF16)
            wd_bf[...] = wd_ref[0].astype(BF16)

        x = _rms(_tok_load(xbuf.at[s], tr, d), nf_ref[...]).astype(BF16)
        g = _dot(x, wg_bf[...])
        u = _dot(x, wu_bf[...])
        hid = (g * (1.0 / (1.0 + jnp.exp(-g))) * u).astype(BF16)
        y = _dot(hid, wd_bf[...])
        gather_rows(i + 1, 1 - s)
        scatter_rows(i, 1 - s)

        @pl.when(i >= 1)
        def _():
            wait_scatter(s)

        _tok_store(ybuf.at[s], y)

        @pl.when(i == nv - 1)
        def _():
            scatter_rows(i + 1, s)
            wait_gather(1 - s)
            wait_scatter(1 - s)
            wait_scatter(s)

    @pl.when(i < nv)
    def _():
        @pl.when(i == 0)
        def _():
            ybuf[...] = jnp.zeros(ybuf.shape, F32)
            for s in range(3):
                pad = pltpu.make_async_copy(ybuf.at[0], y_hbm.at[pl.ds((n_pairs + s * tr) * c, tr * c)],
                                            ssem.at[0])
                pad.start()
                pad.wait()
            gather_rows(0, 0)

        @pl.when(i % 2 == 0)
        def _():
            step(0)

        @pl.when(i % 2 == 1)
        def _():
            step(1)


def _experts_call(tile_expert, n_valid, src, dst, h, nf, wg, wu, wd, layer, n_tiles):
    d, f = wg.shape[-2], wg.shape[-1]
    c = d // LANES
    n = h.shape[0] // c
    tr = ROUTE_TILE
    n_pairs = 2 * n
    w_spec = lambda shape: pl.BlockSpec(shape, lambda i, te, nv, s, t: (layer, te[i], 0, 0))
    grid_spec = pltpu.PrefetchScalarGridSpec(
        num_scalar_prefetch=4,
        grid=(n_tiles,),
        in_specs=[pl.BlockSpec(memory_space=pl.ANY),
                  pl.BlockSpec((1, d), lambda i, te, nv, s, t: (0, 0)),
                  w_spec((None, 1, d, f)), w_spec((None, 1, d, f)), w_spec((None, 1, f, d))],
        out_specs=pl.BlockSpec(memory_space=pl.ANY),
        scratch_shapes=[pltpu.VMEM((2, tr * c, LANES), F32), pltpu.VMEM((2, tr * c, LANES), F32),
                        pltpu.VMEM((d, f), BF16), pltpu.VMEM((d, f), BF16), pltpu.VMEM((f, d), BF16),
                        pltpu.SemaphoreType.DMA((2,)), pltpu.SemaphoreType.DMA((2,))],
    )
    return pl.pallas_call(
        functools.partial(_experts_kernel, tr=tr, d=d, n_pairs=n_pairs),
        grid_spec=grid_spec,
        out_shape=jax.ShapeDtypeStruct(((n_pairs + 3 * tr) * c, LANES), F32),
        compiler_params=_cparams(1),
        name="moe_experts",
    )(tile_expert, n_valid, src, dst, h, nf, wg, wu, wd)


def _combine(h, slab, ya, yb):
    return h + (slab[:, 2:3] * ya + slab[:, 3:4] * yb)


def _kvq_kernel(h_ref, slab_ref, ya_ref, yb_ref, nkv_ref, wk_ref, wv_ref, wf_ref, bf_ref, nb_ref, wq_ref,
                h2_ref, kp_ref, vp_ref, ks_ref, vs_ref, kb_ref, vb_ref, lf_ref, q_ref,
                *, tm, d, p_tiles, qs_prompt, qs_sample):
    prompt = pl.program_id(0) < p_tiles
    h2 = _combine(_tok_load(h_ref, tm, d), slab_ref[...], _tok_load(ya_ref, tm, d), _tok_load(yb_ref, tm, d))
    _tok_store(h2_ref, h2)
    s = _rms(h2, nkv_ref[...]).astype(BF16)
    k = _dot(s, wk_ref[...])
    v = _dot(s, wv_ref[...])

    @pl.when(prompt)
    def _():
        kp_ref[...] = k
        vp_ref[...] = v

    @pl.when(jnp.logical_not(prompt))
    def _():
        ks_ref[...] = k
        vs_ref[...] = v

    kb_ref[...] = k.astype(BF16)
    vb_ref[...] = v.astype(BF16)
    z = _dot(s, wf_ref[...]) + bf_ref[...]
    lf_ref[...] = -(jnp.maximum(-z, 0.0) + jnp.log1p(jnp.exp(-jnp.abs(z))))
    qn = _rms(h2, nb_ref[...]).astype(BF16)
    q_scale = jnp.where(prompt, qs_prompt, qs_sample)
    q_ref[...] = (_dot(qn, wq_ref[...]) * q_scale).astype(BF16)


def _kvq_call(h, slab, y2, nkv, wk, wv, wf, bf, nb, wq, p_tiles, qs_prompt, qs_sample):
    d, hd = wk.shape
    n = h.shape[0] // (d // LANES)
    tm = TOKEN_TILE
    nt = n // tm
    np_ = p_tiles * tm
    tok = lambda w: pl.BlockSpec((tm, w), lambda i: (i, 0))
    res = _tok_spec(tm, d, lambda i: (i, 0))
    p_out = pl.BlockSpec((tm, hd), lambda i: (jnp.minimum(i, p_tiles - 1), 0))
    s_out = pl.BlockSpec((tm, hd), lambda i: (jnp.maximum(i - p_tiles, 0), 0))
    return pl.pallas_call(
        functools.partial(_kvq_kernel, tm=tm, d=d, p_tiles=p_tiles, qs_prompt=qs_prompt, qs_sample=qs_sample),
        grid=(nt,),
        in_specs=[res, tok(LANES), res, _tok_spec(tm, d, lambda i: (i + nt, 0)),
                  _const_spec(nkv.shape), _const_spec(wk.shape), _const_spec(wv.shape), _const_spec(wf.shape),
                  _const_spec(bf.shape), _const_spec(nb.shape), _const_spec(wq.shape)],
        out_specs=[res, p_out, p_out, s_out, s_out, tok(hd), tok(hd), tok(LANES), tok(hd)],
        out_shape=[jax.ShapeDtypeStruct(h.shape, F32), jax.ShapeDtypeStruct((np_, hd), F32),
                   jax.ShapeDtypeStruct((np_, hd), F32), jax.ShapeDtypeStruct((n - np_, hd), F32),
                   jax.ShapeDtypeStruct((n - np_, hd), F32), jax.ShapeDtypeStruct((n, hd), BF16),
                   jax.ShapeDtypeStruct((n, hd), BF16), jax.ShapeDtypeStruct((n, LANES), F32),
                   jax.ShapeDtypeStruct((n, hd), BF16)],
        compiler_params=_cparams(1),
        name="combine_kvq",
    )(h, slab, y2, y2, nkv, wk, wv, wf, bf, nb, wq)


def _final_kernel(h_ref, slab_ref, ya_ref, yb_ref, nfin_ref, yp_ref, ys_ref, *, tm, d, p_tiles):
    h4 = _combine(_tok_load(h_ref, tm, d), slab_ref[...], _tok_load(ya_ref, tm, d), _tok_load(yb_ref, tm, d))
    y = _rms(h4, nfin_ref[...])
    prompt = pl.program_id(0) < p_tiles

    @pl.when(prompt)
    def _():
        yp_ref[...] = y

    @pl.when(jnp.logical_not(prompt))
    def _():
        ys_ref[...] = y


def _final_call(h, slab, y2, nfin, p_tiles):
    d = nfin.shape[1]
    n = h.shape[0] // (d // LANES)
    tm = TOKEN_TILE
    nt = n // tm
    np_ = p_tiles * tm
    res = _tok_spec(tm, d, lambda i: (i, 0))
    return pl.pallas_call(
        functools.partial(_final_kernel, tm=tm, d=d, p_tiles=p_tiles),
        grid=(nt,),
        in_specs=[res, pl.BlockSpec((tm, LANES), lambda i: (i, 0)), res, _tok_spec(tm, d, lambda i: (i + nt, 0)),
                  _const_spec(nfin.shape)],
        out_specs=[pl.BlockSpec((tm, d), lambda i: (jnp.minimum(i, p_tiles - 1), 0)),
                   pl.BlockSpec((tm, d), lambda i: (jnp.maximum(i - p_tiles, 0), 0))],
        out_shape=[jax.ShapeDtypeStruct((np_, d), F32), jax.ShapeDtypeStruct((n - np_, d), F32)],
        compiler_params=_cparams(1),
        name="combine_final",
    )(h, slab, y2, y2, nfin)


def _cumsum_kernel(x_ref, c_ref, *, t):
    rb = x_ref.shape[0]
    row = lax.broadcasted_iota(I32, (LANES, LANES), 0)
    col = lax.broadcasted_iota(I32, (LANES, LANES), 1)
    tri = jnp.where(row <= col, 1.0, 0.0).astype(BF16)
    carry = jnp.zeros((rb, 1), F32)
    for j in range(t // LANES):
        blk = x_ref[:, j * LANES:(j + 1) * LANES]
        a1 = blk.astype(BF16)
        r1 = blk - a1.astype(F32)
        a2 = r1.astype(BF16)
        a3 = (r1 - a2.astype(F32)).astype(BF16)
        cs = _dot(a1, tri) + _dot(a2, tri) + _dot(a3, tri) + carry
        c_ref[:, j * LANES:(j + 1) * LANES] = cs
        carry = cs[:, LANES - 1:LANES]


def _cumsum_call(x):
    r, t = x.shape
    rb = min(r, 128)
    return pl.pallas_call(
        functools.partial(_cumsum_kernel, t=t),
        grid=(r // rb,),
        in_specs=[pl.BlockSpec((rb, t), lambda i: (i, 0))],
        out_specs=pl.BlockSpec((rb, t), lambda i: (i, 0)),
        out_shape=jax.ShapeDtypeStruct((r, t), F32),
        compiler_params=_cparams(1),
        name="logf_cumsum",
    )(x)


def _bias_columns(c2, base, lane, query_side):
    p1 = c2.astype(BF16).astype(F32)
    r1 = c2 - p1
    p2 = r1.astype(BF16).astype(F32)
    p3 = r1 - p2
    off = lane - base
    if query_side:
        ones = (off >= 3) & (off < 6)
        return jnp.where(off == 0, p1, jnp.where(off == 1, p2, jnp.where(off == 2, p3, jnp.where(ones, 1.0, 0.0))))
    ones = (off >= 0) & (off < 3)
    return jnp.where(off == 3, -p1, jnp.where(off == 4, -p2, jnp.where(off == 5, -p3, jnp.where(ones, 1.0, 0.0))))


def _attn_prompt_kernel(q_ref, k_ref, v_ref, c_ref, o_ref, kx0, kx1, vx0, vx1, *, blk, dh, nkv):
    hp = pl.program_id(1)
    qi = pl.program_id(2)
    kx = (kx0, kx1)
    vx = (vx0, vx1)
    lane = lax.broadcasted_iota(I32, (blk, 2 * dh), 1)
    in_head = [(lane >= hd_ * dh) & (lane < (hd_ + 1) * dh) for hd_ in range(2)]
    ext_base = [dh, 0]

    def column(cs, h):
        return jnp.sum(jnp.where(lane == h, cs, 0.0), axis=-1, keepdims=True) * LOG2E

    @pl.when(qi == 0)
    def _():
        for jc in range(nkv):
            rows = slice(jc * blk, (jc + 1) * blk)
            kb = k_ref[rows, :]
            vb = v_ref[rows, :]
            cs = c_ref[rows, :]
            for hd_ in range(2):
                ext = _bias_columns(column(cs, 2 * hp + hd_), ext_base[hd_], lane, False)
                kx[hd_][rows, :] = jnp.where(in_head[hd_], kb, ext.astype(BF16))
                vx[hd_][rows, :] = jnp.where(in_head[hd_], vb, jnp.ones_like(vb))

    q = q_ref[...]
    cs_q = c_ref[pl.ds(pl.multiple_of(qi * blk, blk), blk), :]
    qx = []
    for hd_ in range(2):
        ext = _bias_columns(column(cs_q, 2 * hp + hd_), ext_base[hd_], lane, True)
        qx.append(jnp.where(in_head[hd_], q, ext.astype(BF16)))
    row = lax.broadcasted_iota(I32, (blk, blk), 0)
    col = lax.broadcasted_iota(I32, (blk, blk), 1)
    causal = col <= row

    def scores(j):
        start = pl.multiple_of(j * blk, blk)
        return tuple(_dot_nt(qx[hd_], kx[hd_][pl.ds(start, blk), :]) for hd_ in range(2))

    def accumulate(j, s_pair, carry, masked):
        start = pl.multiple_of(j * blk, blk)
        out = []
        for hd_ in range(2):
            m, acc = carry[hd_]
            s = s_pair[hd_]
            if masked:
                s = jnp.where(causal, s, -jnp.inf)
            m_new = jnp.maximum(m, jnp.max(s, axis=-1, keepdims=True))
            p = jnp.exp2(s - m_new)
            acc = jnp.exp2(m - m_new) * acc + _dot(p.astype(BF16), vx[hd_][pl.ds(start, blk), :])
            out.append((m_new, acc))
        return tuple(out)

    def body(j, state):
        s_cur, carry = state
        s_next = scores(j + 1)
        return s_next, accumulate(j, s_cur, carry, False)

    init = tuple((jnp.full((blk, 1), -jnp.inf, F32), jnp.zeros((blk, 2 * dh), F32)) for _ in range(2))
    s_last, carry = lax.fori_loop(0, qi, body, (scores(0), init))
    (_, acc0), (_, acc1) = accumulate(qi, s_last, carry, True)
    o0 = acc0 / pltpu.roll(acc0, dh, 1)
    o1 = acc1 / pltpu.roll(acc1, dh, 1)
    o_ref[...] = jnp.where(in_head[0], o0, o1).astype(BF16)


def _attn_prompt_call(q, kb, vb, c_slab, batch, seq, n_heads, dh):
    blk = min(ATTN_BLOCK, seq)
    nq = seq // blk
    hd = n_heads * dh
    pair = pl.BlockSpec((seq, 2 * dh), lambda b, hp, qi: (b, hp))
    return pl.pallas_call(
        functools.partial(_attn_prompt_kernel, blk=blk, dh=dh, nkv=nq),
        grid=(batch, n_heads // 2, nq),
        in_specs=[pl.BlockSpec((blk, 2 * dh), lambda b, hp, qi: (b * nq + qi, hp)), pair, pair,
                  pl.BlockSpec((seq, LANES), lambda b, hp, qi: (b, 0))],
        out_specs=pl.BlockSpec((blk, 2 * dh), lambda b, hp, qi: (b * nq + qi, hp)),
        out_shape=jax.ShapeDtypeStruct((batch * seq, hd), BF16),
        scratch_shapes=[pltpu.VMEM((seq, 2 * dh), BF16)] * 4,
        compiler_params=_cparams(3),
        name="attn_prompt",
    )(q, kb, vb, c_slab)


def _attn_sample_kernel(q_ref, kc_ref, vc_ref, kn_ref, vn_ref, cq_ref, ck_ref, o_ref,
                        qbd, m_ref, l_ref, acc_ref, *, ts, n_heads, dh, nchunk):
    j = pl.program_id(1)
    r = n_heads * ts
    hd = n_heads * dh

    @pl.when(j == 0)
    def _():
        q = q_ref[...]
        qt = jnp.concatenate([q] * n_heads, axis=0)
        rr = lax.broadcasted_iota(I32, (r, hd), 0) // ts
        ll = lax.broadcasted_iota(I32, (r, hd), 1) // dh
        qbd[...] = jnp.where(rr == ll, qt, jnp.zeros_like(qt))
        m_ref[...] = jnp.full((r, 1), -jnp.inf, F32)
        l_ref[...] = jnp.zeros((r, 1), F32)
        acc_ref[...] = jnp.zeros((r, hd), F32)

    def expand(ck):
        w = ck.shape[1]
        return jnp.concatenate([jnp.broadcast_to(ck[h:h + 1, :], (ts, w)) for h in range(n_heads)], axis=0)

    def update(s, vb):
        m = m_ref[...]
        m_new = jnp.maximum(m, jnp.max(s, axis=-1, keepdims=True))
        p = jnp.exp(s - m_new)
        alpha = jnp.exp(m - m_new)
        l_ref[...] = alpha * l_ref[...] + jnp.sum(p, axis=-1, keepdims=True)
        acc_ref[...] = alpha * acc_ref[...] + _dot(p.astype(BF16), vb)
        m_ref[...] = m_new

    cq = cq_ref[...][:, 0:1]

    @pl.when(j < nchunk)
    def _():
        kb = kc_ref[...].astype(BF16)
        s = _dot_nt(qbd[...], kb) + (cq - expand(ck_ref[...]))
        update(s, vc_ref[...].astype(BF16))

    @pl.when(j == nchunk)
    def _():
        s = _dot_nt(qbd[...], kn_ref[...]) + (cq - expand(ck_ref[:, 0:ts]))
        qpos = lax.broadcasted_iota(I32, (r, ts), 0) % ts
        kpos = lax.broadcasted_iota(I32, (r, ts), 1)
        s = jnp.where(kpos <= qpos, s, -jnp.inf)
        update(s, vn_ref[...])
        o = acc_ref[...] / l_ref[...]
        ll = lax.broadcasted_iota(I32, (ts, hd), 1) // dh
        out = jnp.zeros((ts, hd), F32)
        for h in range(n_heads):
            out = jnp.where(ll == h, o[h * ts:(h + 1) * ts, :], out)
        o_ref[...] = out.astype(BF16)


def _attn_sample_call(q, cache_k, cache_v, k_new, v_new, cq_rep, ck_chunks, row0, n_streams, ts, n_heads, dh):
    past = cache_k.shape[1]
    hd = n_heads * dh
    ck_len = min(CACHE_CHUNK, past)
    nchunk = past // ck_len
    r = n_heads * ts
    b0 = row0 // ts
    cache_spec = pl.BlockSpec((None, ck_len, hd), lambda b, j: (b, jnp.minimum(j, nchunk - 1), 0))
    new_spec = pl.BlockSpec((ts, hd), lambda b, j: (b0 + b, 0))
    return pl.pallas_call(
        functools.partial(_attn_sample_kernel, ts=ts, n_heads=n_heads, dh=dh, nchunk=nchunk),
        grid=(n_streams, nchunk + 1),
        in_specs=[new_spec, cache_spec, cache_spec, new_spec, new_spec,
                  pl.BlockSpec((r, LANES), lambda b, j: (b, 0)),
                  pl.BlockSpec((n_heads, ck_len), lambda b, j: (b * (nchunk + 1) + j, 0))],
        out_specs=pl.BlockSpec((ts, hd), lambda b, j: (b, 0)),
        out_shape=jax.ShapeDtypeStruct((n_streams * ts, hd), BF16),
        scratch_shapes=[pltpu.VMEM((r, hd), BF16), pltpu.VMEM((r, 1), F32), pltpu.VMEM((r, 1), F32),
                        pltpu.VMEM((r, hd), F32)],
        compiler_params=_cparams(2),
        name="attn_sample",
    )(q, cache_k, cache_v, k_new, v_new, cq_rep, ck_chunks)


def _wo_kernel(h_ref, op_ref, os_ref, wo_ref, nf_ref, wr_ref, brt_ref, h3_ref, slab_ref, meta_ref, cnt_ref,
               carry_ref, *, tm, d, p_tiles, n_groups, epg):
    i = pl.program_id(0)

    @pl.when(i == 0)
    def _():
        carry_ref[...] = jnp.zeros(carry_ref.shape, F32)

    o = jnp.where(i >= p_tiles, os_ref[...], op_ref[...])
    h3 = _tok_load(h_ref, tm, d) + _dot(o, wo_ref[...])
    _tok_store(h3_ref, h3)
    slab_ref[...], meta_ref[...] = _route_tail(h3, nf_ref[...], wr_ref, brt_ref[...], carry_ref, n_groups, epg)
    cnt_ref[...] = carry_ref[...]


def _wo_call(h, o_p, o_s, wo, nf, wr, brt, n_groups, epg):
    hd, d = wo.shape
    n = h.shape[0] // (d // LANES)
    tm = TOKEN_TILE
    p_tiles = o_p.shape[0] // tm
    res = _tok_spec(tm, d, lambda i: (i, 0))
    return pl.pallas_call(
        functools.partial(_wo_kernel, tm=tm, d=d, p_tiles=p_tiles, n_groups=n_groups, epg=epg),
        grid=(n // tm,),
        in_specs=[res,
                  pl.BlockSpec((tm, hd), lambda i: (jnp.minimum(i, p_tiles - 1), 0)),
                  pl.BlockSpec((tm, hd), lambda i: (jnp.maximum(i - p_tiles, 0), 0)),
                  _const_spec(wo.shape), _const_spec(nf.shape), _const_spec(wr.shape), _const_spec(brt.shape)],
        out_specs=[res, pl.BlockSpec((tm, LANES), lambda i: (i, 0)),
                   pl.BlockSpec((SUBLANES, tm), lambda i: (i, 0)), pl.BlockSpec((1, LANES), lambda i: (0, 0))],
        out_shape=[jax.ShapeDtypeStruct(h.shape, F32), jax.ShapeDtypeStruct((n, LANES), F32),
                   jax.ShapeDtypeStruct((n // tm * SUBLANES, tm), F32), jax.ShapeDtypeStruct((1, LANES), F32)],
        scratch_shapes=[pltpu.VMEM((1, LANES), F32)],
        compiler_params=_cparams(1),
        name="wo_router",
    )(h, o_p, o_s, wo, nf, wr, brt)


def _router_weights(w_group, b_group, w_router, b_router):
    d = w_group.shape[0]
    n = w_group.shape[1] + w_router.shape[1]
    w = jnp.zeros((d, LANES), F32).at[:, :n].set(jnp.concatenate([w_group, w_router], axis=1))
    w1 = w.astype(BF16)
    w2 = (w - w1.astype(F32)).astype(BF16)
    b = jnp.zeros((1, LANES), F32).at[0, :n].set(jnp.concatenate([b_group, b_router]))
    return jnp.stack([w1, w2]), b


def _moe(h, meta, counts, nf, wg, wu, wd, layer, n_groups, n_experts):
    tile = ROUTE_TILE
    c = wg.shape[-2] // LANES
    n = h.shape[0] // c
    n_pairs = 2 * n
    cnt = counts[0, n_groups:n_groups + n_experts].astype(I32)
    padded = ((cnt + tile - 1) // tile) * tile
    ends = jnp.cumsum(padded)
    offs = ends - padded
    n_tiles = (n_pairs + n_experts * (tile - 1) + tile - 1) // tile
    n_valid = (ends[-1] // tile).astype(I32)
    starts = jnp.arange(n_tiles, dtype=I32) * tile
    te = jnp.zeros((n_tiles,), I32)
    for e in range(n_experts - 1):
        te = te + (starts >= ends[e]).astype(I32)
    te = jnp.where(jnp.arange(n_tiles) < n_valid, te, te[jnp.maximum(n_valid - 1, 0)])
    m = meta.reshape(-1, SUBLANES, meta.shape[1]).astype(I32)
    ids = m[:, 0:2, :].transpose(1, 0, 2).reshape(2, n)
    ranks = m[:, 4:6, :].transpose(1, 0, 2).reshape(2, n)
    base = jnp.zeros((2, n), I32)
    for e in range(n_experts):
        base = jnp.where(ids == e, offs[e], base)
    pos = (base + ranks).reshape(n_pairs)
    n_slots = (n_tiles + 1) * tile
    inv = _invert_call(pos, n_slots)
    filled = inv >= 0
    slot_id = jnp.arange(n_slots, dtype=I32)
    src = jnp.where(filled, jnp.where(inv >= n, inv - n, inv), 0)
    dst = jnp.where(filled, inv, n_pairs + slot_id % (2 * tile))
    dst = jnp.concatenate([n_pairs + 2 * tile + slot_id[:tile], dst[:n_tiles * tile]])
    return _experts_call(te, n_valid.reshape(1), src * c, dst * c, h, nf, wg, wu, wd, layer, n_tiles)


def kernel(x_prompt, x_sample, state_conv, cache_k, cache_v, cache_logf, norm_a, w_in_a, conv_w_a, w_out_a,
           norm_kv, w_k, w_v, w_f, b_f, norm_b, w_q_b, w_o_b, norm_ffn, w_group, b_group, w_router, b_router,
           w_gate, w_up, w_down, norm_final):
    bp, tp, d = x_prompt.shape
    bs, ts, _ = x_sample.shape
    past, n_heads, dh = cache_k.shape[1], cache_k.shape[2], cache_k.shape[3]
    hd = n_heads * dh
    n_groups = w_group.shape[-1]
    n_experts = w_gate.shape[1]
    epg = n_experts // n_groups
    np_, ns = bp * tp, bs * ts
    tm = TOKEN_TILE
    assert state_conv.shape[0] == 1 and w_q_b.shape[0] == 1 and state_conv.shape[2] == 2
    assert dh * 2 == LANES and n_heads % 2 == 0
    assert tp % tm == 0 and ns % tm == 0 and tm % ts == 0 and ts >= 2
    p_tiles = np_ // tm

    row = lambda a: a.reshape(1, -1).astype(F32)
    win = w_in_a[0].astype(BF16)
    wout = w_out_a[0].astype(BF16)
    wk, wv, wq, wo = w_k.astype(BF16), w_v.astype(BF16), w_q_b[0].astype(BF16), w_o_b[0].astype(BF16)
    wf = jnp.zeros((d, LANES), F32).at[:, :n_heads].set(w_f).astype(BF16)
    bf = jnp.zeros((1, LANES), F32).at[0, :n_heads].set(b_f)
    wr0, br0 = _router_weights(w_group[0], b_group[0], w_router[0], b_router[0])
    wr1, br1 = _router_weights(w_group[1], b_group[1], w_router[1], b_router[1])

    st = state_conv[0]
    s1 = jnp.zeros((bs, ts, d), F32).at[:, 0].set(st[:, 1]).reshape(ns, d)
    s2 = jnp.zeros((bs, ts, d), F32).at[:, 0].set(st[:, 0]).at[:, 1].set(st[:, 1]).reshape(ns, d)

    h1, slab0, meta0, tails, cus, cnt0 = _mixer_call(
        x_prompt.reshape(np_, d), x_sample.reshape(ns, d), s1, s2, tp, ts, row(norm_a[0]), win, conv_w_a[0], wout,
        row(norm_ffn[0]), wr0, br0, n_groups, epg)
    conv_prompt = tails[:p_tiles * SUBLANES].reshape(bp, tp // tm, SUBLANES, d)[:, -1, SUBLANES - 2:][None]
    conv_sample = cus.reshape(bs, ts, d)[:, ts - 2:][None]
    y2 = _moe(h1, meta0, cnt0, row(norm_ffn[0]), w_gate, w_up, w_down, 0, n_groups, n_experts)

    h2, k_p, v_p, k_s, v_s, kb_all, vb_all, lf_all, q_all = _kvq_call(
        h1, slab0, y2, row(norm_kv), wk, wv, wf, bf, row(norm_b[0]), wq, p_tiles,
        float(dh) ** -0.5 * LOG2E, float(dh) ** -0.5)
    logf_prompt = lf_all[:np_, :n_heads].reshape(bp, tp, n_heads)
    logf_sample = lf_all[np_:, :n_heads].reshape(bs, ts, n_heads)

    c_p = _cumsum_call(logf_prompt.transpose(0, 2, 1).reshape(bp * n_heads, tp))
    c_slab = jnp.zeros((np_, LANES), F32).at[:, :n_heads].set(
        c_p.reshape(bp, n_heads, tp).transpose(0, 2, 1).reshape(np_, n_heads))
    o_p = _attn_prompt_call(q_all, kb_all, vb_all, c_slab, bp, tp, n_heads, dh)

    tall = past + ts
    tpad = ((tall + LANES - 1) // LANES) * LANES
    lfs = jnp.concatenate([cache_logf.astype(F32), logf_sample], axis=1).transpose(0, 2, 1)
    lfs = jnp.pad(lfs, ((0, 0), (0, 0), (0, tpad - tall))).reshape(bs * n_heads, tpad)
    c_s = _cumsum_call(lfs).reshape(bs, n_heads, tpad)
    ck_len = min(CACHE_CHUNK, past)
    nchunk = past // ck_len
    c_past = c_s[:, :, :past].reshape(bs, n_heads, nchunk, ck_len).transpose(0, 2, 1, 3)
    c_new = c_s[:, :, past:past + ts]
    c_new_pad = jnp.pad(c_new, ((0, 0), (0, 0), (0, ck_len - ts)))[:, None]
    ck_s = jnp.concatenate([c_past, c_new_pad], axis=1).reshape(bs * (nchunk + 1) * n_heads, ck_len)
    cq_s = jnp.broadcast_to(c_new.reshape(bs * n_heads * ts, 1), (bs * n_heads * ts, LANES))
    o_s = _attn_sample_call(q_all, cache_k.reshape(bs, past, hd), cache_v.reshape(bs, past, hd), kb_all, vb_all,
                            cq_s, ck_s, np_, bs, ts, n_heads, dh)

    h3, slab1, meta1, cnt1 = _wo_call(h2, o_p, o_s, wo, row(norm_ffn[1]), wr1, br1, n_groups, epg)
    y2b = _moe(h3, meta1, cnt1, row(norm_ffn[1]), w_gate, w_up, w_down, 1, n_groups, n_experts)
    y_p, y_s = _final_call(h3, slab1, y2b, row(norm_final), p_tiles)

    return (y_p.reshape(bp, tp, d), y_s.reshape(bs, ts, d), conv_prompt, conv_sample,
            k_p.reshape(bp, tp, n_heads, dh), v_p.reshape(bp, tp, n_heads, dh), logf_prompt,
            k_s.reshape(bs, ts, n_heads, dh), v_s.reshape(bs, ts, n_heads, dh), logf_sample)
```

```python
import functools

import jax
import jax.numpy as jnp
from jax import lax
from jax.experimental import pallas as pl
from jax.experimental.pallas import tpu as pltpu

F32 = jnp.float32
BF16 = jnp.bfloat16
I32 = jnp.int32

RMS_EPS = 1e-6
LOG2E = 1.4426950408889634
LANES = 128
SUBLANES = 8
ROUTE_TILE = 256
TOKEN_TILE = 512
ATTN_BLOCK = 512
CACHE_CHUNK = 512
DMA_UNROLL = 8
VMEM_LIMIT = 56 * 1024 * 1024


def _cparams(n_axes):
    return pltpu.CompilerParams(dimension_semantics=("arbitrary",) * n_axes,
                                vmem_limit_bytes=VMEM_LIMIT)


def _rms(x, g):
    return x * lax.rsqrt(jnp.mean(x * x, axis=-1, keepdims=True) + RMS_EPS) * g


def _dot(a, b):
    return jnp.dot(a, b, preferred_element_type=F32)


def _dot_nt(a, b):
    return lax.dot_general(a, b, (((1,), (1,)), ((), ())), preferred_element_type=F32)


def _tok_load(ref, rows, d):
    c = d // LANES
    return jnp.concatenate([ref[pl.ds(j, rows, stride=c), :] for j in range(c)], axis=1)


def _tok_store(ref, val):
    rows, d = val.shape
    c = d // LANES
    for j in range(c):
        ref[pl.ds(j, rows, stride=c), :] = val[:, j * LANES:(j + 1) * LANES]


def _tok_spec(rows, d, index_map):
    return pl.BlockSpec((rows * (d // LANES), LANES), index_map)


def _const_spec(shape):
    nd = len(shape)
    return pl.BlockSpec(shape, lambda *_: (0,) * nd, pipeline_mode=pl.Buffered(1))


def _route_tail(h, g_ffn, wr_ref, b_rt, carry_ref, n_groups, epg):
    tm = h.shape[0]
    hn = _rms(h, g_ffn)
    a1 = hn.astype(BF16)
    a2 = (hn - a1.astype(F32)).astype(BF16)
    w1 = wr_ref[0]
    w2 = wr_ref[1]
    logits = _dot(a1, w1) + _dot(a1, w2) + _dot(a2, w1) + b_rt
    lane = lax.broadcasted_iota(I32, logits.shape, 1)
    lanef = lane.astype(F32)
    neg = -jnp.inf
    big = 1e9
    gl = jnp.where(lane < n_groups, logits, neg)
    gmax = jnp.max(gl, axis=-1, keepdims=True)
    g_idx = jnp.min(jnp.where(gl == gmax, lanef, big), axis=-1, keepdims=True)
    g_w = 1.0 / jnp.sum(jnp.exp(gl - gmax), axis=-1, keepdims=True)
    lo = n_groups + g_idx * epg
    el = jnp.where((lanef >= lo) & (lanef < lo + epg), logits, neg)
    m1 = jnp.max(el, axis=-1, keepdims=True)
    i1 = jnp.min(jnp.where(el == m1, lanef, big), axis=-1, keepdims=True)
    el2 = jnp.where(lanef == i1, neg, el)
    m2 = jnp.max(el2, axis=-1, keepdims=True)
    i2 = jnp.min(jnp.where(el2 == m2, lanef, big), axis=-1, keepdims=True)
    t = jnp.exp(m2 - m1)
    cw1 = g_w * (1.0 / (1.0 + t))
    cw2 = g_w * (t / (1.0 + t))
    sel1 = lanef == i1
    sel2 = lanef == i2
    oh = jnp.where(sel1 | sel2, 1.0, 0.0)
    row = lax.broadcasted_iota(I32, (tm, tm), 0)
    col = lax.broadcasted_iota(I32, (tm, tm), 1)
    tri = jnp.where(col < row, 1.0, 0.0).astype(BF16)
    carry = carry_ref[...]
    prefix = _dot(tri, oh.astype(BF16)) + carry
    rank1 = jnp.sum(jnp.where(sel1, prefix, 0.0), axis=-1, keepdims=True)
    rank2 = jnp.sum(jnp.where(sel2, prefix, 0.0), axis=-1, keepdims=True)
    carry_ref[...] = carry + jnp.sum(oh, axis=0, keepdims=True)
    slab = jnp.where(lane == 0, i1 - n_groups,
           jnp.where(lane == 1, i2 - n_groups,
           jnp.where(lane == 2, cw1,
           jnp.where(lane == 3, cw2,
           jnp.where(lane == 4, rank1,
           jnp.where(lane == 5, rank2, 0.0))))))
    return slab, jnp.transpose(slab)[0:SUBLANES, :]


def _mixer_kernel(xp_ref, xs_ref, s1_ref, s2_ref, na_ref, win_ref, cw_ref, wout_ref, nf_ref, wr_ref, brt_ref,
                  h_ref, slab_ref, meta_ref, tail_ref, cus_ref, cnt_ref, cubuf, carry_ref,
                  *, tm, d, p_tiles, tiles_per_seq, s_len, n_groups, epg):
    i = pl.program_id(0)
    sample = i >= p_tiles

    @pl.when(i == 0)
    def _():
        carry_ref[...] = jnp.zeros(carry_ref.shape, F32)

    @pl.when(sample | (i % tiles_per_seq == 0))
    def _():
        cubuf[0:SUBLANES, :] = jnp.zeros((SUBLANES, d), F32)

    x = jnp.where(sample, xs_ref[...], xp_ref[...])
    xn = _rms(x, na_ref[...]).astype(BF16)
    bcu = _dot(xn, win_ref[...])
    b_gate = bcu[:, 0:d]
    cu = bcu[:, d:2 * d] * bcu[:, 2 * d:3 * d]
    cubuf[SUBLANES:SUBLANES + tm, :] = cu
    prev1 = cubuf[SUBLANES - 1:SUBLANES - 1 + tm, :]
    prev2 = cubuf[SUBLANES - 2:SUBLANES - 2 + tm, :]
    r = lax.broadcasted_iota(I32, (tm, d), 0) % s_len
    prev1 = jnp.where(sample & (r == 0), s1_ref[...], prev1)
    prev2 = jnp.where(sample & (r < 2), s2_ref[...], prev2)
    tail = cubuf[tm:tm + SUBLANES, :]
    cubuf[0:SUBLANES, :] = tail
    tail_ref[...] = tail

    @pl.when(sample)
    def _():
        cus_ref[...] = cu

    cw = cw_ref[...]
    y = cw[0:1, :] * prev2 + cw[1:2, :] * prev1 + cw[2:3, :] * cu
    h = x + _dot((b_gate * y).astype(BF16), wout_ref[...])
    _tok_store(h_ref, h)
    slab_ref[...], meta_ref[...] = _route_tail(h, nf_ref[...], wr_ref, brt_ref[...], carry_ref, n_groups, epg)
    cnt_ref[...] = carry_ref[...]


def _mixer_call(xp, xs, s1, s2, seq_len, s_len, na, win, cw, wout, nf, wr, brt, n_groups, epg):
    np_, d = xp.shape
    ns = xs.shape[0]
    tm = TOKEN_TILE
    p_tiles, s_tiles = np_ // tm, ns // tm
    n_tiles = p_tiles + s_tiles
    n = np_ + ns
    p_spec = pl.BlockSpec((tm, d), lambda i: (jnp.minimum(i, p_tiles - 1), 0))
    s_spec = pl.BlockSpec((tm, d), lambda i: (jnp.maximum(i - p_tiles, 0), 0))
    return pl.pallas_call(
        functools.partial(_mixer_kernel, tm=tm, d=d, p_tiles=p_tiles, tiles_per_seq=seq_len // tm, s_len=s_len,
                          n_groups=n_groups, epg=epg),
        grid=(n_tiles,),
        in_specs=[p_spec, s_spec, s_spec, s_spec, _const_spec(na.shape), _const_spec(win.shape),
                  _const_spec(cw.shape), _const_spec(wout.shape), _const_spec(nf.shape), _const_spec(wr.shape),
                  _const_spec(brt.shape)],
        out_specs=[_tok_spec(tm, d, lambda i: (i, 0)), pl.BlockSpec((tm, LANES), lambda i: (i, 0)),
                   pl.BlockSpec((SUBLANES, tm), lambda i: (i, 0)),
                   pl.BlockSpec((SUBLANES, d), lambda i: (i, 0)), s_spec,
                   pl.BlockSpec((1, LANES), lambda i: (0, 0))],
        out_shape=[jax.ShapeDtypeStruct((n * (d // LANES), LANES), F32), jax.ShapeDtypeStruct((n, LANES), F32),
                   jax.ShapeDtypeStruct((n_tiles * SUBLANES, tm), F32),
                   jax.ShapeDtypeStruct((n_tiles * SUBLANES, d), F32), jax.ShapeDtypeStruct((ns, d), F32),
                   jax.ShapeDtypeStruct((1, LANES), F32)],
        scratch_shapes=[pltpu.VMEM((tm + SUBLANES, d), F32), pltpu.VMEM((1, LANES), F32)],
        compiler_params=_cparams(1),
        name="mixer_a",
    )(xp, xs, s1, s2, na, win, cw, wout, nf, wr, brt)


def _invert_kernel(pos_ref, empty_hbm, inv_ref, *, n_pairs):
    pltpu.sync_copy(empty_hbm, inv_ref)

    def fill(p, c):
        inv_ref[pos_ref[p]] = p
        return c

    lax.fori_loop(0, n_pairs, fill, 0, unroll=DMA_UNROLL)


def _invert_call(pos, n_slots):
    n_pairs = pos.shape[0]
    grid_spec = pltpu.PrefetchScalarGridSpec(
        num_scalar_prefetch=1, grid=(1,), in_specs=[pl.BlockSpec(memory_space=pl.ANY)],
        out_specs=pl.BlockSpec(memory_space=pltpu.SMEM))
    return pl.pallas_call(
        functools.partial(_invert_kernel, n_pairs=n_pairs),
        grid_spec=grid_spec,
        out_shape=jax.ShapeDtypeStruct((n_slots,), I32),
        compiler_params=_cparams(1),
        name="invert_perm",
    )(pos, jnp.full((n_slots,), -1, I32))


def _experts_kernel(te_ref, nv_ref, src_ref, dst_ref, h_hbm, nf_ref, wg_ref, wu_ref, wd_ref, y_hbm,
                    xbuf, ybuf, wg_bf, wu_bf, wd_bf, gsem, ssem, *, tr, d, n_pairs):
    i = pl.program_id(0)
    nv = nv_ref[0]
    c = d // LANES

    def gather_rows(tile, s):
        base = tile * tr
        for r in range(tr):
            row0 = pl.multiple_of(src_ref[base + r], c)
            pltpu.make_async_copy(h_hbm.at[pl.ds(row0, c)], xbuf.at[s, pl.ds(r * c, c)],
                                  gsem.at[s]).start(priority=r % 2)

    def scatter_rows(dst_tile, s):
        base = dst_tile * tr
        for r in range(tr):
            row0 = pl.multiple_of(dst_ref[base + r], c)
            pltpu.make_async_copy(ybuf.at[s, pl.ds(r * c, c)], y_hbm.at[pl.ds(row0, c)],
                                  ssem.at[s]).start(priority=r % 2)

    def wait_gather(s):
        pltpu.make_async_copy(h_hbm.at[pl.ds(0, tr * c)], xbuf.at[s], gsem.at[s]).wait()

    def wait_scatter(s):
        pltpu.make_async_copy(ybuf.at[s], y_hbm.at[pl.ds(0, tr * c)], ssem.at[s]).wait()

    def step(s):
        wait_gather(s)
        prev = te_ref[jnp.maximum(i - 1, 0)]

        @pl.when((i == 0) | (te_ref[i] != prev))
        def _():
            wg_bf[...] = wg_ref[0].astype(BF16)
            wu_bf[...] = wu_ref[0].astype(BF16)
            wd_bf[...] = wd_ref[0].astype(BF16)

        x = _rms(_tok_load(xbuf.at[s], tr, d), nf_ref[...]).astype(BF16)
        g = _dot(x, wg_bf[...])
        u = _dot(x, wu_bf[...])
        hid = (g * (1.0 / (1.0 + jnp.exp(-g))) * u).astype(BF16)
        y = _dot(hid, wd_bf[...])
        gather_rows(i + 1, 1 - s)
        scatter_rows(i, 1 - s)

        @pl.when(i >= 1)
        def _():
            wait_scatter(s)

        _tok_store(ybuf.at[s], y)

        @pl.when(i == nv - 1)
        def _():
            scatter_rows(i + 1, s)
            wait_gather(1 - s)
            wait_scatter(1 - s)
            wait_scatter(s)

    @pl.when(i < nv)
    def _():
        @pl.when(i == 0)
        def _():
            ybuf[...] = jnp.zeros(ybuf.shape, F32)
            for s in range(3):
                pad = pltpu.make_async_copy(ybuf.at[0], y_hbm.at[pl.ds((n_pairs + s * tr) * c, tr * c)],
                                            ssem.at[0])
                pad.start()
                pad.wait()
            gather_rows(0, 0)

        @pl.when(i % 2 == 0)
        def _():
            step(0)

        @pl.when(i % 2 == 1)
        def _():
            step(1)


def _experts_call(tile_expert, n_valid, src, dst, h, nf, wg, wu, wd, layer, n_tiles):
    d, f = wg.shape[-2], wg.shape[-1]
    c = d // LANES
    n = h.shape[0] // c
    tr = ROUTE_TILE
    n_pairs = 2 * n
    w_spec = lambda shape: pl.BlockSpec(shape, lambda i, te, nv, s, t: (layer, te[i], 0, 0))
    grid_spec = pltpu.PrefetchScalarGridSpec(
        num_scalar_prefetch=4,
        grid=(n_tiles,),
        in_specs=[pl.BlockSpec(memory_space=pl.ANY),
                  pl.BlockSpec((1, d), lambda i, te, nv, s, t: (0, 0)),
                  w_spec((None, 1, d, f)), w_spec((None, 1, d, f)), w_spec((None, 1, f, d))],
        out_specs=pl.BlockSpec(memory_space=pl.ANY),
        scratch_shapes=[pltpu.VMEM((2, tr * c, LANES), F32), pltpu.VMEM((2, tr * c, LANES), F32),
                        pltpu.VMEM((d, f), BF16), pltpu.VMEM((d, f), BF16), pltpu.VMEM((f, d), BF16),
                        pltpu.SemaphoreType.DMA((2,)), pltpu.SemaphoreType.DMA((2,))],
    )
    return pl.pallas_call(
        functools.partial(_experts_kernel, tr=tr, d=d, n_pairs=n_pairs),
        grid_spec=grid_spec,
        out_shape=jax.ShapeDtypeStruct(((n_pairs + 3 * tr) * c, LANES), F32),
        compiler_params=_cparams(1),
        name="moe_experts",
    )(tile_expert, n_valid, src, dst, h, nf, wg, wu, wd)


def _combine(h, slab, ya, yb):
    return h + (slab[:, 2:3] * ya + slab[:, 3:4] * yb)


def _kvq_kernel(h_ref, slab_ref, ya_ref, yb_ref, nkv_ref, wk_ref, wv_ref, wf_ref, bf_ref, nb_ref, wq_ref,
                h2_ref, kp_ref, vp_ref, ks_ref, vs_ref, kb_ref, vb_ref, lf_ref, q_ref,
                *, tm, d, p_tiles, qs_prompt, qs_sample):
    prompt = pl.program_id(0) < p_tiles
    h2 = _combine(_tok_load(h_ref, tm, d), slab_ref[...], _tok_load(ya_ref, tm, d), _tok_load(yb_ref, tm, d))
    _tok_store(h2_ref, h2)
    s = _rms(h2, nkv_ref[...]).astype(BF16)
    k = _dot(s, wk_ref[...])
    v = _dot(s, wv_ref[...])

    @pl.when(prompt)
    def _():
        kp_ref[...] = k
        vp_ref[...] = v

    @pl.when(jnp.logical_not(prompt))
    def _():
        ks_ref[...] = k
        vs_ref[...] = v

    kb_ref[...] = k.astype(BF16)
    vb_ref[...] = v.astype(BF16)
    z = _dot(s, wf_ref[...]) + bf_ref[...]
    lf_ref[...] = -(jnp.maximum(-z, 0.0) + jnp.log1p(jnp.exp(-jnp.abs(z))))
    qn = _rms(h2, nb_ref[...]).astype(BF16)
    q_scale = jnp.where(prompt, qs_prompt, qs_sample)
    q_ref[...] = (_dot(qn, wq_ref[...]) * q_scale).astype(BF16)


def _kvq_call(h, slab, y2, nkv, wk, wv, wf, bf, nb, wq, p_tiles, qs_prompt, qs_sample):
    d, hd = wk.shape
    n = h.shape[0] // (d // LANES)
    tm = TOKEN_TILE
    nt = n // tm
    np_ = p_tiles * tm
    tok = lambda w: pl.BlockSpec((tm, w), lambda i: (i, 0))
    res = _tok_spec(tm, d, lambda i: (i, 0))
    p_out = pl.BlockSpec((tm, hd), lambda i: (jnp.minimum(i, p_tiles - 1), 0))
    s_out = pl.BlockSpec((tm, hd), lambda i: (jnp.maximum(i - p_tiles, 0), 0))
    return pl.pallas_call(
        functools.partial(_kvq_kernel, tm=tm, d=d, p_tiles=p_tiles, qs_prompt=qs_prompt, qs_sample=qs_sample),
        grid=(nt,),
        in_specs=[res, tok(LANES), res, _tok_spec(tm, d, lambda i: (i + nt, 0)),
                  _const_spec(nkv.shape), _const_spec(wk.shape), _const_spec(wv.shape), _const_spec(wf.shape),
                  _const_spec(bf.shape), _const_spec(nb.shape), _const_spec(wq.shape)],
        out_specs=[res, p_out, p_out, s_out, s_out, tok(hd), tok(hd), tok(LANES), tok(hd)],
        out_shape=[jax.ShapeDtypeStruct(h.shape, F32), jax.ShapeDtypeStruct((np_, hd), F32),
                   jax.ShapeDtypeStruct((np_, hd), F32), jax.ShapeDtypeStruct((n - np_, hd), F32),
                   jax.ShapeDtypeStruct((n - np_, hd), F32), jax.ShapeDtypeStruct((n, hd), BF16),
                   jax.ShapeDtypeStruct((n, hd), BF16), jax.ShapeDtypeStruct((n, LANES), F32),
                   jax.ShapeDtypeStruct((n, hd), BF16)],
        compiler_params=_cparams(1),
        name="combine_kvq",
    )(h, slab, y2, y2, nkv, wk, wv, wf, bf, nb, wq)


def _final_kernel(h_ref, slab_ref, ya_ref, yb_ref, nfin_ref, yp_ref, ys_ref, *, tm, d, p_tiles):
    h4 = _combine(_tok_load(h_ref, tm, d), slab_ref[...], _tok_load(ya_ref, tm, d), _tok_load(yb_ref, tm, d))
    y = _rms(h4, nfin_ref[...])
    prompt = pl.program_id(0) < p_tiles

    @pl.when(prompt)
    def _():
        yp_ref[...] = y

    @pl.when(jnp.logical_not(prompt))
    def _():
        ys_ref[...] = y


def _final_call(h, slab, y2, nfin, p_tiles):
    d = nfin.shape[1]
    n = h.shape[0] // (d // LANES)
    tm = TOKEN_TILE
    nt = n // tm
    np_ = p_tiles * tm
    res = _tok_spec(tm, d, lambda i: (i, 0))
    return pl.pallas_call(
        functools.partial(_final_kernel, tm=tm, d=d, p_tiles=p_tiles),
        grid=(nt,),
        in_specs=[res, pl.BlockSpec((tm, LANES), lambda i: (i, 0)), res, _tok_spec(tm, d, lambda i: (i + nt, 0)),
                  _const_spec(nfin.shape)],
        out_specs=[pl.BlockSpec((tm, d), lambda i: (jnp.minimum(i, p_tiles - 1), 0)),
                   pl.BlockSpec((tm, d), lambda i: (jnp.maximum(i - p_tiles, 0), 0))],
        out_shape=[jax.ShapeDtypeStruct((np_, d), F32), jax.ShapeDtypeStruct((n - np_, d), F32)],
        compiler_params=_cparams(1),
        name="combine_final",
    )(h, slab, y2, y2, nfin)


def _cumsum_kernel(x_ref, c_ref, *, t):
    rb = x_ref.shape[0]
    row = lax.broadcasted_iota(I32, (LANES, LANES), 0)
    col = lax.broadcasted_iota(I32, (LANES, LANES), 1)
    tri = jnp.where(row <= col, 1.0, 0.0).astype(BF16)
    carry = jnp.zeros((rb, 1), F32)
    for j in range(t // LANES):
        blk = x_ref[:, j * LANES:(j + 1) * LANES]
        a1 = blk.astype(BF16)
        r1 = blk - a1.astype(F32)
        a2 = r1.astype(BF16)
        a3 = (r1 - a2.astype(F32)).astype(BF16)
        cs = _dot(a1, tri) + _dot(a2, tri) + _dot(a3, tri) + carry
        c_ref[:, j * LANES:(j + 1) * LANES] = cs
        carry = cs[:, LANES - 1:LANES]


def _cumsum_call(x):
    r, t = x.shape
    rb = min(r, 128)
    return pl.pallas_call(
        functools.partial(_cumsum_kernel, t=t),
        grid=(r // rb,),
        in_specs=[pl.BlockSpec((rb, t), lambda i: (i, 0))],
        out_specs=pl.BlockSpec((rb, t), lambda i: (i, 0)),
        out_shape=jax.ShapeDtypeStruct((r, t), F32),
        compiler_params=_cparams(1),
        name="logf_cumsum",
    )(x)


def _bias_columns(c2, base, lane, query_side):
    p1 = c2.astype(BF16).astype(F32)
    r1 = c2 - p1
    p2 = r1.astype(BF16).astype(F32)
    p3 = r1 - p2
    off = lane - base
    if query_side:
        ones = (off >= 3) & (off < 6)
        return jnp.where(off == 0, p1, jnp.where(off == 1, p2, jnp.where(off == 2, p3, jnp.where(ones, 1.0, 0.0))))
    ones = (off >= 0) & (off < 3)
    return jnp.where(off == 3, -p1, jnp.where(off == 4, -p2, jnp.where(off == 5, -p3, jnp.where(ones, 1.0, 0.0))))


def _attn_prompt_kernel(q_ref, k_ref, v_ref, c_ref, o_ref, kx0, kx1, vx0, vx1, *, blk, dh, nkv):
    hp = pl.program_id(1)
    qi = pl.program_id(2)
    kx = (kx0, kx1)
    vx = (vx0, vx1)
    lane = lax.broadcasted_iota(I32, (blk, 2 * dh), 1)
    in_head = [(lane >= hd_ * dh) & (lane < (hd_ + 1) * dh) for hd_ in range(2)]
    ext_base = [dh, 0]

    def column(cs, h):
        return jnp.sum(jnp.where(lane == h, cs, 0.0), axis=-1, keepdims=True) * LOG2E

    @pl.when(qi == 0)
    def _():
        for jc in range(nkv):
            rows = slice(jc * blk, (jc + 1) * blk)
            kb = k_ref[rows, :]
            vb = v_ref[rows, :]
            cs = c_ref[rows, :]
            for hd_ in range(2):
                ext = _bias_columns(column(cs, 2 * hp + hd_), ext_base[hd_], lane, False)
                kx[hd_][rows, :] = jnp.where(in_head[hd_], kb, ext.astype(BF16))
                vx[hd_][rows, :] = jnp.where(in_head[hd_], vb, jnp.ones_like(vb))

    q = q_ref[...]
    cs_q = c_ref[pl.ds(pl.multiple_of(qi * blk, blk), blk), :]
    qx = []
    for hd_ in range(2):
        ext = _bias_columns(column(cs_q, 2 * hp + hd_), ext_base[hd_], lane, True)
        qx.append(jnp.where(in_head[hd_], q, ext.astype(BF16)))
    row = lax.broadcasted_iota(I32, (blk, blk), 0)
    col = lax.broadcasted_iota(I32, (blk, blk), 1)
    causal = col <= row

    def scores(j):
        start = pl.multiple_of(j * blk, blk)
        return tuple(_dot_nt(qx[hd_], kx[hd_][pl.ds(start, blk), :]) for hd_ in range(2))

    def accumulate(j, s_pair, carry, masked):
        start = pl.multiple_of(j * blk, blk)
        out = []
        for hd_ in range(2):
            m, acc = carry[hd_]
            s = s_pair[hd_]
            if masked:
                s = jnp.where(causal, s, -jnp.inf)
            m_new = jnp.maximum(m, jnp.max(s, axis=-1, keepdims=True))
            p = jnp.exp2(s - m_new)
            acc = jnp.exp2(m - m_new) * acc + _dot(p.astype(BF16), vx[hd_][pl.ds(start, blk), :])
            out.append((m_new, acc))
        return tuple(out)

    def body(j, state):
        s_cur, carry = state
        s_next = scores(j + 1)
        return s_next, accumulate(j, s_cur, carry, False)

    init = tuple((jnp.full((blk, 1), -jnp.inf, F32), jnp.zeros((blk, 2 * dh), F32)) for _ in range(2))
    s_last, carry = lax.fori_loop(0, qi, body, (scores(0), init))
    (_, acc0), (_, acc1) = accumulate(qi, s_last, carry, True)
    o0 = acc0 / pltpu.roll(acc0, dh, 1)
    o1 = acc1 / pltpu.roll(acc1, dh, 1)
    o_ref[...] = jnp.where(in_head[0], o0, o1).astype(BF16)


def _attn_prompt_call(q, kb, vb, c_slab, batch, seq, n_heads, dh):
    blk = min(ATTN_BLOCK, seq)
    nq = seq // blk
    hd = n_heads * dh
    pair = pl.BlockSpec((seq, 2 * dh), lambda b, hp, qi: (b, hp))
    return pl.pallas_call(
        functools.partial(_attn_prompt_kernel, blk=blk, dh=dh, nkv=nq),
        grid=(batch, n_heads // 2, nq),
        in_specs=[pl.BlockSpec((blk, 2 * dh), lambda b, hp, qi: (b * nq + qi, hp)), pair, pair,
                  pl.BlockSpec((seq, LANES), lambda b, hp, qi: (b, 0))],
        out_specs=pl.BlockSpec((blk, 2 * dh), lambda b, hp, qi: (b * nq + qi, hp)),
        out_shape=jax.ShapeDtypeStruct((batch * seq, hd), BF16),
        scratch_shapes=[pltpu.VMEM((seq, 2 * dh), BF16)] * 4,
        compiler_params=_cparams(3),
        name="attn_prompt",
    )(q, kb, vb, c_slab)


def _attn_sample_kernel(q_ref, kc_ref, vc_ref, kn_ref, vn_ref, cq_ref, ckp_ref, ckn_ref, o_ref,
                        qf, m_ref, l_ref, acc_ref, *, ts, n_heads, dh, nchunk, hg):
    j = pl.program_id(1)
    rg = hg * ts

    @pl.when(j == 0)
    def _():
        q = q_ref[...]
        for h in range(n_heads):
            qf[h * ts:(h + 1) * ts, :] = q[:, h * dh:(h + 1) * dh]
        m_ref[...] = jnp.full(m_ref.shape, -jnp.inf, F32)
        l_ref[...] = jnp.zeros(l_ref.shape, F32)
        acc_ref[...] = jnp.zeros(acc_ref.shape, F32)

    def block(k_ref, v_ref, bias_ref, keys, causal):
        w = keys * hg
        if causal or w % LANES or LANES % hg:
            pw = w
        else:
            pw = LANES
        row = lax.broadcasted_iota(I32, (rg, pw), 0)
        col = lax.broadcasted_iota(I32, (rg, pw), 1)
        keep = (col % hg) == (row // ts)
        if causal:
            keep = keep & ((col // hg) <= (row % ts))
        for g in range(n_heads // hg):
            rows = slice(g * rg, (g + 1) * rg)
            kf = k_ref[:, g * hg:(g + 1) * hg, :].reshape(w, dh).astype(BF16)
            vf = v_ref[:, g * hg:(g + 1) * hg, :].reshape(w, dh).astype(BF16)
            pat = jnp.where(keep, 0.0, -jnp.inf) + cq_ref[rows, 0:1]
            if pw != w:
                pat = jnp.concatenate([pat] * (w // pw), axis=1)
            s = _dot_nt(qf[rows, :], kf) + pat - bias_ref[g]
            m = m_ref[rows, :]
            m_new = jnp.maximum(m, jnp.max(s, axis=-1, keepdims=True))
            p = jnp.exp(s - m_new)
            alpha = jnp.exp(m - m_new)
            l_ref[rows, :] = alpha * l_ref[rows, :] + jnp.sum(p, axis=-1, keepdims=True)
            acc_ref[rows, :] = alpha * acc_ref[rows, :] + _dot(p.astype(BF16), vf)
            m_ref[rows, :] = m_new

    @pl.when(j < nchunk)
    def _():
        block(kc_ref, vc_ref, ckp_ref, kc_ref.shape[0], False)

    @pl.when(j == nchunk)
    def _():
        block(kn_ref, vn_ref, ckn_ref, ts, True)
        for h in range(n_heads):
            rows = slice(h * ts, (h + 1) * ts)
            o_ref[:, h * dh:(h + 1) * dh] = (acc_ref[rows, :] / l_ref[rows, :]).astype(BF16)


def _attn_sample_call(q, cache_k, cache_v, k_new, v_new, cq_rep, ck_past, ck_new, row0, ts, hg):
    n_streams, past, n_heads, dh = cache_k.shape
    hd = n_heads * dh
    ck_len = min(CACHE_CHUNK, past)
    nchunk = past // ck_len
    r = n_heads * ts
    b0 = row0 // ts
    ng = n_heads // hg
    cache_spec = pl.BlockSpec((None, ck_len, n_heads, dh), lambda b, j: (b, jnp.minimum(j, nchunk - 1), 0, 0))
    new_spec = pl.BlockSpec((None, ts, n_heads, dh), lambda b, j: (b, 0, 0, 0))
    return pl.pallas_call(
        functools.partial(_attn_sample_kernel, ts=ts, n_heads=n_heads, dh=dh, nchunk=nchunk, hg=hg),
        grid=(n_streams, nchunk + 1),
        in_specs=[pl.BlockSpec((ts, hd), lambda b, j: (b0 + b, 0)), cache_spec, cache_spec, new_spec, new_spec,
                  pl.BlockSpec((r, LANES), lambda b, j: (b, 0)),
                  pl.BlockSpec((None, ng, 1, ck_len * hg),
                               lambda b, j: (b * nchunk + jnp.minimum(j, nchunk - 1), 0, 0, 0)),
                  pl.BlockSpec((None, ng, 1, ts * hg), lambda b, j: (b, 0, 0, 0))],
        out_specs=pl.BlockSpec((ts, hd), lambda b, j: (b, 0)),
        out_shape=jax.ShapeDtypeStruct((n_streams * ts, hd), BF16),
        scratch_shapes=[pltpu.VMEM((r, dh), BF16), pltpu.VMEM((r, 1), F32), pltpu.VMEM((r, 1), F32),
                        pltpu.VMEM((r, dh), F32)],
        compiler_params=_cparams(2),
        name="attn_sample",
    )(q, cache_k, cache_v, k_new, v_new, cq_rep, ck_past, ck_new)


def _wo_kernel(h_ref, op_ref, os_ref, wo_ref, nf_ref, wr_ref, brt_ref, h3_ref, slab_ref, meta_ref, cnt_ref,
               carry_ref, *, tm, d, p_tiles, n_groups, epg):
    i = pl.program_id(0)

    @pl.when(i == 0)
    def _():
        carry_ref[...] = jnp.zeros(carry_ref.shape, F32)

    o = jnp.where(i >= p_tiles, os_ref[...], op_ref[...])
    h3 = _tok_load(h_ref, tm, d) + _dot(o, wo_ref[...])
    _tok_store(h3_ref, h3)
    slab_ref[...], meta_ref[...] = _route_tail(h3, nf_ref[...], wr_ref, brt_ref[...], carry_ref, n_groups, epg)
    cnt_ref[...] = carry_ref[...]


def _wo_call(h, o_p, o_s, wo, nf, wr, brt, n_groups, epg):
    hd, d = wo.shape
    n = h.shape[0] // (d // LANES)
    tm = TOKEN_TILE
    p_tiles = o_p.shape[0] // tm
    res = _tok_spec(tm, d, lambda i: (i, 0))
    return pl.pallas_call(
        functools.partial(_wo_kernel, tm=tm, d=d, p_tiles=p_tiles, n_groups=n_groups, epg=epg),
        grid=(n // tm,),
        in_specs=[res,
                  pl.BlockSpec((tm, hd), lambda i: (jnp.minimum(i, p_tiles - 1), 0)),
                  pl.BlockSpec((tm, hd), lambda i: (jnp.maximum(i - p_tiles, 0), 0)),
                  _const_spec(wo.shape), _const_spec(nf.shape), _const_spec(wr.shape), _const_spec(brt.shape)],
        out_specs=[res, pl.BlockSpec((tm, LANES), lambda i: (i, 0)),
                   pl.BlockSpec((SUBLANES, tm), lambda i: (i, 0)), pl.BlockSpec((1, LANES), lambda i: (0, 0))],
        out_shape=[jax.ShapeDtypeStruct(h.shape, F32), jax.ShapeDtypeStruct((n, LANES), F32),
                   jax.ShapeDtypeStruct((n // tm * SUBLANES, tm), F32), jax.ShapeDtypeStruct((1, LANES), F32)],
        scratch_shapes=[pltpu.VMEM((1, LANES), F32)],
        compiler_params=_cparams(1),
        name="wo_router",
    )(h, o_p, o_s, wo, nf, wr, brt)


def _router_weights(w_group, b_group, w_router, b_router):
    d = w_group.shape[0]
    n = w_group.shape[1] + w_router.shape[1]
    w = jnp.zeros((d, LANES), F32).at[:, :n].set(jnp.concatenate([w_group, w_router], axis=1))
    w1 = w.astype(BF16)
    w2 = (w - w1.astype(F32)).astype(BF16)
    b = jnp.zeros((1, LANES), F32).at[0, :n].set(jnp.concatenate([b_group, b_router]))
    return jnp.stack([w1, w2]), b


def _moe(h, meta, counts, nf, wg, wu, wd, layer, n_groups, n_experts):
    tile = ROUTE_TILE
    c = wg.shape[-2] // LANES
    n = h.shape[0] // c
    n_pairs = 2 * n
    cnt = counts[0, n_groups:n_groups + n_experts].astype(I32)
    padded = ((cnt + tile - 1) // tile) * tile
    ends = jnp.cumsum(padded)
    offs = ends - padded
    n_tiles = (n_pairs + n_experts * (tile - 1) + tile - 1) // tile
    n_valid = (ends[-1] // tile).astype(I32)
    starts = jnp.arange(n_tiles, dtype=I32) * tile
    te = jnp.zeros((n_tiles,), I32)
    for e in range(n_experts - 1):
        te = te + (starts >= ends[e]).astype(I32)
    te = jnp.where(jnp.arange(n_tiles) < n_valid, te, te[jnp.maximum(n_valid - 1, 0)])
    m = meta.reshape(-1, SUBLANES, meta.shape[1]).astype(I32)
    ids = m[:, 0:2, :].transpose(1, 0, 2).reshape(2, n)
    ranks = m[:, 4:6, :].transpose(1, 0, 2).reshape(2, n)
    base = jnp.zeros((2, n), I32)
    for e in range(n_experts):
        base = jnp.where(ids == e, offs[e], base)
    pos = (base + ranks).reshape(n_pairs)
    n_slots = (n_tiles + 1) * tile
    inv = _invert_call(pos, n_slots)
    filled = inv >= 0
    slot_id = jnp.arange(n_slots, dtype=I32)
    src = jnp.where(filled, jnp.where(inv >= n, inv - n, inv), 0)
    dst = jnp.where(filled, inv, n_pairs + slot_id % (2 * tile))
    dst = jnp.concatenate([n_pairs + 2 * tile + slot_id[:tile], dst[:n_tiles * tile]])
    return _experts_call(te, n_valid.reshape(1), src * c, dst * c, h, nf, wg, wu, wd, layer, n_tiles)


def kernel(x_prompt, x_sample, state_conv, cache_k, cache_v, cache_logf, norm_a, w_in_a, conv_w_a, w_out_a,
           norm_kv, w_k, w_v, w_f, b_f, norm_b, w_q_b, w_o_b, norm_ffn, w_group, b_group, w_router, b_router,
           w_gate, w_up, w_down, norm_final):
    bp, tp, d = x_prompt.shape
    bs, ts, _ = x_sample.shape
    past, n_heads, dh = cache_k.shape[1], cache_k.shape[2], cache_k.shape[3]
    hd = n_heads * dh
    n_groups = w_group.shape[-1]
    n_experts = w_gate.shape[1]
    epg = n_experts // n_groups
    np_, ns = bp * tp, bs * ts
    tm = TOKEN_TILE
    assert state_conv.shape[0] == 1 and w_q_b.shape[0] == 1 and state_conv.shape[2] == 2
    assert dh * 2 == LANES and n_heads % 2 == 0
    assert tp % tm == 0 and ns % tm == 0 and tm % ts == 0 and ts >= 2
    p_tiles = np_ // tm

    row = lambda a: a.reshape(1, -1).astype(F32)
    win = w_in_a[0].astype(BF16)
    wout = w_out_a[0].astype(BF16)
    wk, wv, wq, wo = w_k.astype(BF16), w_v.astype(BF16), w_q_b[0].astype(BF16), w_o_b[0].astype(BF16)
    wf = jnp.zeros((d, LANES), F32).at[:, :n_heads].set(w_f).astype(BF16)
    bf = jnp.zeros((1, LANES), F32).at[0, :n_heads].set(b_f)
    wr0, br0 = _router_weights(w_group[0], b_group[0], w_router[0], b_router[0])
    wr1, br1 = _router_weights(w_group[1], b_group[1], w_router[1], b_router[1])

    st = state_conv[0]
    s1 = jnp.zeros((bs, ts, d), F32).at[:, 0].set(st[:, 1]).reshape(ns, d)
    s2 = jnp.zeros((bs, ts, d), F32).at[:, 0].set(st[:, 0]).at[:, 1].set(st[:, 1]).reshape(ns, d)

    h1, slab0, meta0, tails, cus, cnt0 = _mixer_call(
        x_prompt.reshape(np_, d), x_sample.reshape(ns, d), s1, s2, tp, ts, row(norm_a[0]), win, conv_w_a[0], wout,
        row(norm_ffn[0]), wr0, br0, n_groups, epg)
    conv_prompt = tails[:p_tiles * SUBLANES].reshape(bp, tp // tm, SUBLANES, d)[:, -1, SUBLANES - 2:][None]
    conv_sample = cus.reshape(bs, ts, d)[:, ts - 2:][None]
    y2 = _moe(h1, meta0, cnt0, row(norm_ffn[0]), w_gate, w_up, w_down, 0, n_groups, n_experts)

    h2, k_p, v_p, k_s, v_s, kb_all, vb_all, lf_all, q_all = _kvq_call(
        h1, slab0, y2, row(norm_kv), wk, wv, wf, bf, row(norm_b[0]), wq, p_tiles,
        float(dh) ** -0.5 * LOG2E, float(dh) ** -0.5)
    logf_prompt = lf_all[:np_, :n_heads].reshape(bp, tp, n_heads)
    logf_sample = lf_all[np_:, :n_heads].reshape(bs, ts, n_heads)

    c_p = _cumsum_call(logf_prompt.transpose(0, 2, 1).reshape(bp * n_heads, tp))
    c_slab = jnp.zeros((np_, LANES), F32).at[:, :n_heads].set(
        c_p.reshape(bp, n_heads, tp).transpose(0, 2, 1).reshape(np_, n_heads))
    o_p = _attn_prompt_call(q_all, kb_all, vb_all, c_slab, bp, tp, n_heads, dh)

    tall = past + ts
    tpad = ((tall + LANES - 1) // LANES) * LANES
    lfs = jnp.concatenate([cache_logf.astype(F32), logf_sample], axis=1).transpose(0, 2, 1)
    lfs = jnp.pad(lfs, ((0, 0), (0, 0), (0, tpad - tall))).reshape(bs * n_heads, tpad)
    c_s = _cumsum_call(lfs).reshape(bs, n_heads, tpad)
    ck_len = min(CACHE_CHUNK, past)
    nchunk = past // ck_len
    hg = min(SUBLANES, n_heads)
    ng = n_heads // hg
    ck_past = c_s[:, :, :past].reshape(bs, ng, hg, nchunk, ck_len).transpose(0, 3, 1, 4, 2).reshape(
        bs * nchunk, ng, 1, ck_len * hg)
    c_new = c_s[:, :, past:past + ts]
    ck_new = c_new.reshape(bs, ng, hg, ts).transpose(0, 1, 3, 2).reshape(bs, ng, 1, ts * hg)
    cq_s = jnp.broadcast_to(c_new.reshape(bs * n_heads * ts, 1), (bs * n_heads * ts, LANES))
    k_sample = k_s.reshape(bs, ts, n_heads, dh)
    v_sample = v_s.reshape(bs, ts, n_heads, dh)
    o_s = _attn_sample_call(q_all, cache_k, cache_v, k_sample, v_sample, cq_s, ck_past, ck_new, np_, ts, hg)

    h3, slab1, meta1, cnt1 = _wo_call(h2, o_p, o_s, wo, row(norm_ffn[1]), wr1, br1, n_groups, epg)
    y2b = _moe(h3, meta1, cnt1, row(norm_ffn[1]), w_gate, w_up, w_down, 1, n_groups, n_experts)
    y_p, y_s = _final_call(h3, slab1, y2b, row(norm_final), p_tiles)

    return (y_p.reshape(bp, tp, d), y_s.reshape(bs, ts, d), conv_prompt, conv_sample,
            k_p.reshape(bp, tp, n_heads, dh), v_p.reshape(bp, tp, n_heads, dh), logf_prompt,
            k_sample, v_sample, logf_sample)
```

```python
import functools

import jax
import jax.numpy as jnp
from jax import lax
from jax.experimental import pallas as pl
from jax.experimental.pallas import tpu as pltpu

F32 = jnp.float32
BF16 = jnp.bfloat16
I32 = jnp.int32

RMS_EPS = 1e-6
LOG2E = 1.4426950408889634
LANES = 128
SUBLANES = 8
ROUTE_TILE = 256
TOKEN_TILE = 512
KVQ_TILE = 256
ATTN_BLOCK = 512
CACHE_CHUNK = 1024
DMA_UNROLL = 8
VMEM_LIMIT = 56 * 1024 * 1024


def _cparams(n_axes):
    return pltpu.CompilerParams(dimension_semantics=("arbitrary",) * n_axes,
                                vmem_limit_bytes=VMEM_LIMIT)


def _rms(x, g):
    return x * lax.rsqrt(jnp.mean(x * x, axis=-1, keepdims=True) + RMS_EPS) * g


def _dot(a, b):
    return jnp.dot(a, b, preferred_element_type=F32)


def _dot_nt(a, b):
    return lax.dot_general(a, b, (((1,), (1,)), ((), ())), preferred_element_type=F32)


def _tok_load(ref, rows, d):
    c = d // LANES
    return jnp.concatenate([ref[pl.ds(j, rows, stride=c), :] for j in range(c)], axis=1)


def _tok_store(ref, val):
    rows, d = val.shape
    c = d // LANES
    for j in range(c):
        ref[pl.ds(j, rows, stride=c), :] = val[:, j * LANES:(j + 1) * LANES]


def _tok_spec(rows, d, index_map):
    return pl.BlockSpec((rows * (d // LANES), LANES), index_map)


def _const_spec(shape):
    nd = len(shape)
    return pl.BlockSpec(shape, lambda *_: (0,) * nd, pipeline_mode=pl.Buffered(1))


def _route_tail(h, g_ffn, wr_ref, b_rt, carry_ref, n_groups, epg):
    tm = h.shape[0]
    hn = _rms(h, g_ffn)
    a1 = hn.astype(BF16)
    a2 = (hn - a1.astype(F32)).astype(BF16)
    w1 = wr_ref[0]
    w2 = wr_ref[1]
    logits = _dot(a1, w1) + _dot(a1, w2) + _dot(a2, w1) + b_rt
    lane = lax.broadcasted_iota(I32, logits.shape, 1)
    lanef = lane.astype(F32)
    neg = -jnp.inf
    big = 1e9
    gl = jnp.where(lane < n_groups, logits, neg)
    gmax = jnp.max(gl, axis=-1, keepdims=True)
    g_idx = jnp.min(jnp.where(gl == gmax, lanef, big), axis=-1, keepdims=True)
    g_w = 1.0 / jnp.sum(jnp.exp(gl - gmax), axis=-1, keepdims=True)
    lo = n_groups + g_idx * epg
    el = jnp.where((lanef >= lo) & (lanef < lo + epg), logits, neg)
    m1 = jnp.max(el, axis=-1, keepdims=True)
    i1 = jnp.min(jnp.where(el == m1, lanef, big), axis=-1, keepdims=True)
    el2 = jnp.where(lanef == i1, neg, el)
    m2 = jnp.max(el2, axis=-1, keepdims=True)
    i2 = jnp.min(jnp.where(el2 == m2, lanef, big), axis=-1, keepdims=True)
    t = jnp.exp(m2 - m1)
    cw1 = g_w * (1.0 / (1.0 + t))
    cw2 = g_w * (t / (1.0 + t))
    sel1 = lanef == i1
    sel2 = lanef == i2
    oh = jnp.where(sel1 | sel2, 1.0, 0.0)
    row = lax.broadcasted_iota(I32, (tm, tm), 0)
    col = lax.broadcasted_iota(I32, (tm, tm), 1)
    tri = jnp.where(col < row, 1.0, 0.0).astype(BF16)
    carry = carry_ref[...]
    prefix = _dot(tri, oh.astype(BF16)) + carry
    rank1 = jnp.sum(jnp.where(sel1, prefix, 0.0), axis=-1, keepdims=True)
    rank2 = jnp.sum(jnp.where(sel2, prefix, 0.0), axis=-1, keepdims=True)
    carry_ref[...] = carry + jnp.sum(oh, axis=0, keepdims=True)
    slab = jnp.where(lane == 0, i1 - n_groups,
           jnp.where(lane == 1, i2 - n_groups,
           jnp.where(lane == 2, cw1,
           jnp.where(lane == 3, cw2,
           jnp.where(lane == 4, rank1,
           jnp.where(lane == 5, rank2, 0.0))))))
    return slab, jnp.transpose(slab)[0:SUBLANES, :]


def _mixer_kernel(xp_ref, xs_ref, s1_ref, s2_ref, na_ref, win_ref, cw_ref, wout_ref, nf_ref, wr_ref, brt_ref,
                  h_ref, slab_ref, meta_ref, tail_ref, cus_ref, cnt_ref, cubuf, carry_ref,
                  *, tm, d, p_tiles, tiles_per_seq, s_len, n_groups, epg):
    i = pl.program_id(0)
    sample = i >= p_tiles

    @pl.when(i == 0)
    def _():
        carry_ref[...] = jnp.zeros(carry_ref.shape, F32)

    @pl.when(sample | (i % tiles_per_seq == 0))
    def _():
        cubuf[0:SUBLANES, :] = jnp.zeros((SUBLANES, d), F32)

    x = jnp.where(sample, xs_ref[...], xp_ref[...])
    xn = _rms(x, na_ref[...]).astype(BF16)
    bcu = _dot(xn, win_ref[...])
    b_gate = bcu[:, 0:d]
    cu = bcu[:, d:2 * d] * bcu[:, 2 * d:3 * d]
    cubuf[SUBLANES:SUBLANES + tm, :] = cu
    prev1 = cubuf[SUBLANES - 1:SUBLANES - 1 + tm, :]
    prev2 = cubuf[SUBLANES - 2:SUBLANES - 2 + tm, :]
    r = lax.broadcasted_iota(I32, (tm, d), 0) % s_len
    prev1 = jnp.where(sample & (r == 0), s1_ref[...], prev1)
    prev2 = jnp.where(sample & (r < 2), s2_ref[...], prev2)
    tail = cubuf[tm:tm + SUBLANES, :]
    cubuf[0:SUBLANES, :] = tail
    tail_ref[...] = tail

    @pl.when(sample)
    def _():
        cus_ref[...] = cu

    cw = cw_ref[...]
    y = cw[0:1, :] * prev2 + cw[1:2, :] * prev1 + cw[2:3, :] * cu
    h = x + _dot((b_gate * y).astype(BF16), wout_ref[...])
    _tok_store(h_ref, h)
    slab_ref[...], meta_ref[...] = _route_tail(h, nf_ref[...], wr_ref, brt_ref[...], carry_ref, n_groups, epg)
    cnt_ref[...] = carry_ref[...]


def _mixer_call(xp, xs, s1, s2, seq_len, s_len, na, win, cw, wout, nf, wr, brt, n_groups, epg):
    np_, d = xp.shape
    ns = xs.shape[0]
    tm = TOKEN_TILE
    p_tiles, s_tiles = np_ // tm, ns // tm
    n_tiles = p_tiles + s_tiles
    n = np_ + ns
    p_spec = pl.BlockSpec((tm, d), lambda i: (jnp.minimum(i, p_tiles - 1), 0))
    s_spec = pl.BlockSpec((tm, d), lambda i: (jnp.maximum(i - p_tiles, 0), 0))
    return pl.pallas_call(
        functools.partial(_mixer_kernel, tm=tm, d=d, p_tiles=p_tiles, tiles_per_seq=seq_len // tm, s_len=s_len,
                          n_groups=n_groups, epg=epg),
        grid=(n_tiles,),
        in_specs=[p_spec, s_spec, s_spec, s_spec, _const_spec(na.shape), _const_spec(win.shape),
                  _const_spec(cw.shape), _const_spec(wout.shape), _const_spec(nf.shape), _const_spec(wr.shape),
                  _const_spec(brt.shape)],
        out_specs=[_tok_spec(tm, d, lambda i: (i, 0)), pl.BlockSpec((tm, LANES), lambda i: (i, 0)),
                   pl.BlockSpec((SUBLANES, tm), lambda i: (i, 0)),
                   pl.BlockSpec((SUBLANES, d), lambda i: (i, 0)), s_spec,
                   pl.BlockSpec((1, LANES), lambda i: (0, 0))],
        out_shape=[jax.ShapeDtypeStruct((n * (d // LANES), LANES), F32), jax.ShapeDtypeStruct((n, LANES), F32),
                   jax.ShapeDtypeStruct((n_tiles * SUBLANES, tm), F32),
                   jax.ShapeDtypeStruct((n_tiles * SUBLANES, d), F32), jax.ShapeDtypeStruct((ns, d), F32),
                   jax.ShapeDtypeStruct((1, LANES), F32)],
        scratch_shapes=[pltpu.VMEM((tm + SUBLANES, d), F32), pltpu.VMEM((1, LANES), F32)],
        compiler_params=_cparams(1),
        name="mixer_a",
    )(xp, xs, s1, s2, na, win, cw, wout, nf, wr, brt)


def _invert_kernel(pos_ref, empty_hbm, inv_ref, *, n_pairs):
    pltpu.sync_copy(empty_hbm, inv_ref)

    def fill(p, c):
        inv_ref[pos_ref[p]] = p
        return c

    lax.fori_loop(0, n_pairs, fill, 0, unroll=DMA_UNROLL)


def _invert_call(pos, n_slots):
    n_pairs = pos.shape[0]
    grid_spec = pltpu.PrefetchScalarGridSpec(
        num_scalar_prefetch=1, grid=(1,), in_specs=[pl.BlockSpec(memory_space=pl.ANY)],
        out_specs=pl.BlockSpec(memory_space=pltpu.SMEM))
    return pl.pallas_call(
        functools.partial(_invert_kernel, n_pairs=n_pairs),
        grid_spec=grid_spec,
        out_shape=jax.ShapeDtypeStruct((n_slots,), I32),
        compiler_params=_cparams(1),
        name="invert_perm",
    )(pos, jnp.full((n_slots,), -1, I32))


def _experts_kernel(te_ref, nv_ref, src_ref, dst_ref, h_hbm, nf_ref, wg_ref, wu_ref, wd_ref, y_hbm,
                    xbuf, ybuf, wg_bf, wu_bf, wd_bf, gsem, ssem, *, tr, d, n_pairs):
    i = pl.program_id(0)
    nv = nv_ref[0]
    c = d // LANES

    def gather_rows(tile, s):
        base = tile * tr
        for r in range(tr):
            row0 = pl.multiple_of(src_ref[base + r], c)
            pltpu.make_async_copy(h_hbm.at[pl.ds(row0, c)], xbuf.at[s, pl.ds(r * c, c)],
                                  gsem.at[s]).start(priority=r % 2)

    def scatter_rows(dst_tile, s):
        base = dst_tile * tr
        for r in range(tr):
            row0 = pl.multiple_of(dst_ref[base + r], c)
            pltpu.make_async_copy(ybuf.at[s, pl.ds(r * c, c)], y_hbm.at[pl.ds(row0, c)],
                                  ssem.at[s]).start(priority=r % 2)

    def wait_gather(s):
        pltpu.make_async_copy(h_hbm.at[pl.ds(0, tr * c)], xbuf.at[s], gsem.at[s]).wait()

    def wait_scatter(s):
        pltpu.make_async_copy(ybuf.at[s], y_hbm.at[pl.ds(0, tr * c)], ssem.at[s]).wait()

    def step(s):
        wait_gather(s)
        prev = te_ref[jnp.maximum(i - 1, 0)]

        @pl.when((i == 0) | (te_ref[i] != prev))
        def _():
            wg_bf[...] = wg_ref[0].astype(BF16)
            wu_bf[...] = wu_ref[0].astype(BF16)
            wd_bf[...] = wd_ref[0].astype(BF16)

        x = _rms(_tok_load(xbuf.at[s], tr, d), nf_ref[...]).astype(BF16)
        g = _dot(x, wg_bf[...])
        u = _dot(x, wu_bf[...])
        hid = (g * (1.0 / (1.0 + jnp.exp(-g))) * u).astype(BF16)
        y = _dot(hid, wd_bf[...])
        gather_rows(i + 1, 1 - s)
        scatter_rows(i, 1 - s)

        @pl.when(i >= 1)
        def _():
            wait_scatter(s)

        _tok_store(ybuf.at[s], y)

        @pl.when(i == nv - 1)
        def _():
            scatter_rows(i + 1, s)
            wait_gather(1 - s)
            wait_scatter(1 - s)
            wait_scatter(s)

    @pl.when(i < nv)
    def _():
        @pl.when(i == 0)
        def _():
            ybuf[...] = jnp.zeros(ybuf.shape, F32)
            for s in range(3):
                pad = pltpu.make_async_copy(ybuf.at[0], y_hbm.at[pl.ds((n_pairs + s * tr) * c, tr * c)],
                                            ssem.at[0])
                pad.start()
                pad.wait()
            gather_rows(0, 0)

        @pl.when(i % 2 == 0)
        def _():
            step(0)

        @pl.when(i % 2 == 1)
        def _():
            step(1)


def _experts_call(tile_expert, n_valid, src, dst, h, nf, wg, wu, wd, layer, n_tiles):
    d, f = wg.shape[-2], wg.shape[-1]
    c = d // LANES
    n = h.shape[0] // c
    tr = ROUTE_TILE
    n_pairs = 2 * n
    w_spec = lambda shape: pl.BlockSpec(shape, lambda i, te, nv, s, t: (layer, te[i], 0, 0))
    grid_spec = pltpu.PrefetchScalarGridSpec(
        num_scalar_prefetch=4,
        grid=(n_tiles,),
        in_specs=[pl.BlockSpec(memory_space=pl.ANY),
                  pl.BlockSpec((1, d), lambda i, te, nv, s, t: (0, 0)),
                  w_spec((None, 1, d, f)), w_spec((None, 1, d, f)), w_spec((None, 1, f, d))],
        out_specs=pl.BlockSpec(memory_space=pl.ANY),
        scratch_shapes=[pltpu.VMEM((2, tr * c, LANES), F32), pltpu.VMEM((2, tr * c, LANES), F32),
                        pltpu.VMEM((d, f), BF16), pltpu.VMEM((d, f), BF16), pltpu.VMEM((f, d), BF16),
                        pltpu.SemaphoreType.DMA((2,)), pltpu.SemaphoreType.DMA((2,))],
    )
    return pl.pallas_call(
        functools.partial(_experts_kernel, tr=tr, d=d, n_pairs=n_pairs),
        grid_spec=grid_spec,
        out_shape=jax.ShapeDtypeStruct(((n_pairs + 3 * tr) * c, LANES), F32),
        compiler_params=_cparams(1),
        name="moe_experts",
    )(tile_expert, n_valid, src, dst, h, nf, wg, wu, wd)


def _combine(h, slab, ya, yb):
    return h + (slab[:, 2:3] * ya + slab[:, 3:4] * yb)


def _kvq_kernel(h_ref, slab_ref, ya_ref, yb_ref, nkv_ref, wk_ref, wv_ref, wkt_ref, wvt_ref, wf_ref, bf_ref, nb_ref,
                wq_ref, h2_ref, ktp_ref, vtp_ref, ks_ref, vs_ref, kb_ref, vb_ref, lf_ref, q_ref,
                *, tm, d, p_tiles, qs_prompt, qs_sample):
    prompt = pl.program_id(0) < p_tiles
    h2 = _combine(_tok_load(h_ref, tm, d), slab_ref[...], _tok_load(ya_ref, tm, d), _tok_load(yb_ref, tm, d))
    _tok_store(h2_ref, h2)
    s = _rms(h2, nkv_ref[...]).astype(BF16)
    k = _dot(s, wk_ref[...])
    v = _dot(s, wv_ref[...])

    @pl.when(prompt)
    def _():
        ktp_ref[...] = _dot_nt(wkt_ref[...], s)
        vtp_ref[...] = _dot_nt(wvt_ref[...], s)

    @pl.when(jnp.logical_not(prompt))
    def _():
        ks_ref[...] = k
        vs_ref[...] = v

    kb_ref[...] = k.astype(BF16)
    vb_ref[...] = v.astype(BF16)
    z = _dot(s, wf_ref[...]) + bf_ref[...]
    lf_ref[...] = -(jnp.maximum(-z, 0.0) + jnp.log1p(jnp.exp(-jnp.abs(z))))
    qn = _rms(h2, nb_ref[...]).astype(BF16)
    q_scale = jnp.where(prompt, qs_prompt, qs_sample)
    q_ref[...] = (_dot(qn, wq_ref[...]) * q_scale).astype(BF16)


def _kvq_call(h, slab, y2, nkv, wk, wv, wkt, wvt, wf, bf, nb, wq, p_tiles, tiles_per_seq, qs_prompt, qs_sample):
    d, hd = wk.shape
    n = h.shape[0] // (d // LANES)
    tm = KVQ_TILE
    nt = n // tm
    np_ = p_tiles * tm
    tok = lambda w: pl.BlockSpec((tm, w), lambda i: (i, 0))
    res = _tok_spec(tm, d, lambda i: (i, 0))

    def p_map(i):
        ip = jnp.minimum(i, p_tiles - 1)
        return (ip // tiles_per_seq, 0, ip % tiles_per_seq)

    p_out = pl.BlockSpec((None, hd, tm), p_map)
    s_out = pl.BlockSpec((tm, hd), lambda i: (jnp.maximum(i - p_tiles, 0), 0))
    kt_shape = jax.ShapeDtypeStruct((p_tiles // tiles_per_seq, hd, tiles_per_seq * tm), F32)
    return pl.pallas_call(
        functools.partial(_kvq_kernel, tm=tm, d=d, p_tiles=p_tiles, qs_prompt=qs_prompt, qs_sample=qs_sample),
        grid=(nt,),
        in_specs=[res, tok(LANES), res, _tok_spec(tm, d, lambda i: (i + nt, 0)),
                  _const_spec(nkv.shape), _const_spec(wk.shape), _const_spec(wv.shape), _const_spec(wkt.shape),
                  _const_spec(wvt.shape), _const_spec(wf.shape), _const_spec(bf.shape), _const_spec(nb.shape),
                  _const_spec(wq.shape)],
        out_specs=[res, p_out, p_out, s_out, s_out, tok(hd), tok(hd), tok(LANES), tok(hd)],
        out_shape=[jax.ShapeDtypeStruct(h.shape, F32), kt_shape, kt_shape,
                   jax.ShapeDtypeStruct((n - np_, hd), F32),
                   jax.ShapeDtypeStruct((n - np_, hd), F32), jax.ShapeDtypeStruct((n, hd), BF16),
                   jax.ShapeDtypeStruct((n, hd), BF16), jax.ShapeDtypeStruct((n, LANES), F32),
                   jax.ShapeDtypeStruct((n, hd), BF16)],
        compiler_params=_cparams(1),
        name="combine_kvq",
    )(h, slab, y2, y2, nkv, wk, wv, wkt, wvt, wf, bf, nb, wq)


def _final_kernel(h_ref, slab_ref, ya_ref, yb_ref, nfin_ref, yp_ref, ys_ref, *, tm, d, p_tiles):
    h4 = _combine(_tok_load(h_ref, tm, d), slab_ref[...], _tok_load(ya_ref, tm, d), _tok_load(yb_ref, tm, d))
    y = _rms(h4, nfin_ref[...])
    prompt = pl.program_id(0) < p_tiles

    @pl.when(prompt)
    def _():
        yp_ref[...] = y

    @pl.when(jnp.logical_not(prompt))
    def _():
        ys_ref[...] = y


def _final_call(h, slab, y2, nfin, p_tiles):
    d = nfin.shape[1]
    n = h.shape[0] // (d // LANES)
    tm = TOKEN_TILE
    nt = n // tm
    np_ = p_tiles * tm
    res = _tok_spec(tm, d, lambda i: (i, 0))
    return pl.pallas_call(
        functools.partial(_final_kernel, tm=tm, d=d, p_tiles=p_tiles),
        grid=(nt,),
        in_specs=[res, pl.BlockSpec((tm, LANES), lambda i: (i, 0)), res, _tok_spec(tm, d, lambda i: (i + nt, 0)),
                  _const_spec(nfin.shape)],
        out_specs=[pl.BlockSpec((tm, d), lambda i: (jnp.minimum(i, p_tiles - 1), 0)),
                   pl.BlockSpec((tm, d), lambda i: (jnp.maximum(i - p_tiles, 0), 0))],
        out_shape=[jax.ShapeDtypeStruct((np_, d), F32), jax.ShapeDtypeStruct((n - np_, d), F32)],
        compiler_params=_cparams(1),
        name="combine_final",
    )(h, slab, y2, y2, nfin)


def _cumsum_kernel(x_ref, c_ref, *, t):
    rb = x_ref.shape[0]
    row = lax.broadcasted_iota(I32, (LANES, LANES), 0)
    col = lax.broadcasted_iota(I32, (LANES, LANES), 1)
    tri = jnp.where(row <= col, 1.0, 0.0).astype(BF16)
    carry = jnp.zeros((rb, 1), F32)
    for j in range(t // LANES):
        blk = x_ref[:, j * LANES:(j + 1) * LANES]
        a1 = blk.astype(BF16)
        r1 = blk - a1.astype(F32)
        a2 = r1.astype(BF16)
        a3 = (r1 - a2.astype(F32)).astype(BF16)
        cs = _dot(a1, tri) + _dot(a2, tri) + _dot(a3, tri) + carry
        c_ref[:, j * LANES:(j + 1) * LANES] = cs
        carry = cs[:, LANES - 1:LANES]


def _cumsum_call(x):
    r, t = x.shape
    rb = min(r, 128)
    return pl.pallas_call(
        functools.partial(_cumsum_kernel, t=t),
        grid=(r // rb,),
        in_specs=[pl.BlockSpec((rb, t), lambda i: (i, 0))],
        out_specs=pl.BlockSpec((rb, t), lambda i: (i, 0)),
        out_shape=jax.ShapeDtypeStruct((r, t), F32),
        compiler_params=_cparams(1),
        name="logf_cumsum",
    )(x)


def _bias_columns(c2, base, lane, query_side):
    p1 = c2.astype(BF16).astype(F32)
    r1 = c2 - p1
    p2 = r1.astype(BF16).astype(F32)
    p3 = r1 - p2
    off = lane - base
    if query_side:
        ones = (off >= 3) & (off < 6)
        return jnp.where(off == 0, p1, jnp.where(off == 1, p2, jnp.where(off == 2, p3, jnp.where(ones, 1.0, 0.0))))
    ones = (off >= 0) & (off < 3)
    return jnp.where(off == 3, -p1, jnp.where(off == 4, -p2, jnp.where(off == 5, -p3, jnp.where(ones, 1.0, 0.0))))


def _attn_prompt_kernel(q_ref, k_ref, v_ref, c_ref, o_ref, kx0, kx1, vx0, vx1, *, blk, dh, nkv):
    hp = pl.program_id(1)
    qi = pl.program_id(2)
    kx = (kx0, kx1)
    vx = (vx0, vx1)
    lane = lax.broadcasted_iota(I32, (blk, 2 * dh), 1)
    in_head = [(lane >= hd_ * dh) & (lane < (hd_ + 1) * dh) for hd_ in range(2)]
    ext_base = [dh, 0]

    def column(cs, h):
        return jnp.sum(jnp.where(lane == h, cs, 0.0), axis=-1, keepdims=True) * LOG2E

    @pl.when(qi == 0)
    def _():
        for jc in range(nkv):
            rows = slice(jc * blk, (jc + 1) * blk)
            kb = k_ref[rows, :]
            vb = v_ref[rows, :]
            cs = c_ref[rows, :]
            for hd_ in range(2):
                ext = _bias_columns(column(cs, 2 * hp + hd_), ext_base[hd_], lane, False)
                kx[hd_][rows, :] = jnp.where(in_head[hd_], kb, ext.astype(BF16))
                vx[hd_][rows, :] = jnp.where(in_head[hd_], vb, jnp.ones_like(vb))

    q = q_ref[...]
    cs_q = c_ref[pl.ds(pl.multiple_of(qi * blk, blk), blk), :]
    qx = []
    for hd_ in range(2):
        ext = _bias_columns(column(cs_q, 2 * hp + hd_), ext_base[hd_], lane, True)
        qx.append(jnp.where(in_head[hd_], q, ext.astype(BF16)))
    row = lax.broadcasted_iota(I32, (blk, blk), 0)
    col = lax.broadcasted_iota(I32, (blk, blk), 1)
    causal = col <= row

    def scores(j):
        start = pl.multiple_of(j * blk, blk)
        return tuple(_dot_nt(qx[hd_], kx[hd_][pl.ds(start, blk), :]) for hd_ in range(2))

    def accumulate(j, s_pair, carry, masked):
        start = pl.multiple_of(j * blk, blk)
        out = []
        for hd_ in range(2):
            m, acc = carry[hd_]
            s = s_pair[hd_]
            if masked:
                s = jnp.where(causal, s, -jnp.inf)
            m_new = jnp.maximum(m, jnp.max(s, axis=-1, keepdims=True))
            p = jnp.exp2(s - m_new)
            acc = jnp.exp2(m - m_new) * acc + _dot(p.astype(BF16), vx[hd_][pl.ds(start, blk), :])
            out.append((m_new, acc))
        return tuple(out)

    def body(j, state):
        s_cur, carry = state
        s_next = scores(j + 1)
        return s_next, accumulate(j, s_cur, carry, False)

    init = tuple((jnp.full((blk, 1), -jnp.inf, F32), jnp.zeros((blk, 2 * dh), F32)) for _ in range(2))
    s_last, carry = lax.fori_loop(0, qi, body, (scores(0), init))
    (_, acc0), (_, acc1) = accumulate(qi, s_last, carry, True)
    o0 = acc0 / pltpu.roll(acc0, dh, 1)
    o1 = acc1 / pltpu.roll(acc1, dh, 1)
    o_ref[...] = jnp.where(in_head[0], o0, o1).astype(BF16)


def _attn_prompt_call(q, kb, vb, c_slab, batch, seq, n_heads, dh):
    blk = min(ATTN_BLOCK, seq)
    nq = seq // blk
    hd = n_heads * dh
    pair = pl.BlockSpec((seq, 2 * dh), lambda b, hp, qi: (b, hp))
    return pl.pallas_call(
        functools.partial(_attn_prompt_kernel, blk=blk, dh=dh, nkv=nq),
        grid=(batch, n_heads // 2, nq),
        in_specs=[pl.BlockSpec((blk, 2 * dh), lambda b, hp, qi: (b * nq + qi, hp)), pair, pair,
                  pl.BlockSpec((seq, LANES), lambda b, hp, qi: (b, 0))],
        out_specs=pl.BlockSpec((blk, 2 * dh), lambda b, hp, qi: (b * nq + qi, hp)),
        out_shape=jax.ShapeDtypeStruct((batch * seq, hd), BF16),
        scratch_shapes=[pltpu.VMEM((seq, 2 * dh), BF16)] * 4,
        compiler_params=_cparams(3),
        name="attn_prompt",
    )(q, kb, vb, c_slab)


def _attn_sample_kernel(q_ref, kc_ref, vc_ref, kn_ref, vn_ref, cq_ref, ck_ref, o_ref,
                        qbd, m_ref, l_ref, acc_ref, *, ts, n_heads, dh, nchunk):
    j = pl.program_id(1)
    r = n_heads * ts
    hd = n_heads * dh

    @pl.when(j == 0)
    def _():
        q = q_ref[...]
        qt = jnp.concatenate([q] * n_heads, axis=0)
        rr = lax.broadcasted_iota(I32, (r, hd), 0) // ts
        ll = lax.broadcasted_iota(I32, (r, hd), 1) // dh
        qbd[...] = jnp.where(rr == ll, qt, jnp.zeros_like(qt))
        m_ref[...] = jnp.full((r, 1), -jnp.inf, F32)
        l_ref[...] = jnp.zeros((r, 1), F32)
        acc_ref[...] = jnp.zeros((r, hd), F32)

    def expand(ck):
        w = ck.shape[1]
        return jnp.concatenate([jnp.broadcast_to(ck[h:h + 1, :], (ts, w)) for h in range(n_heads)], axis=0)

    def update(s, pv):
        m = m_ref[...]
        m_new = jnp.maximum(m, jnp.max(s, axis=-1, keepdims=True))
        p = jnp.exp(s - m_new)
        alpha = jnp.exp(m - m_new)
        l_ref[...] = alpha * l_ref[...] + jnp.sum(p, axis=-1, keepdims=True)
        acc_ref[...] = alpha * acc_ref[...] + pv(p.astype(BF16))
        m_ref[...] = m_new

    cq = cq_ref[...][:, 0:1]

    @pl.when(j < nchunk)
    def _():
        w = kc_ref.shape[-1]
        kt = kc_ref[...].reshape(hd, w).astype(BF16)
        vt = vc_ref[...].reshape(hd, w).astype(BF16)
        s = _dot(qbd[...], kt) + (cq - expand(ck_ref[...]))
        update(s, lambda p: _dot_nt(p, vt))

    @pl.when(j == nchunk)
    def _():
        s = _dot_nt(qbd[...], kn_ref[...]) + (cq - expand(ck_ref[:, 0:ts]))
        qpos = lax.broadcasted_iota(I32, (r, ts), 0) % ts
        kpos = lax.broadcasted_iota(I32, (r, ts), 1)
        s = jnp.where(kpos <= qpos, s, -jnp.inf)
        update(s, lambda p: _dot(p, vn_ref[...]))
        o = acc_ref[...] / l_ref[...]
        ll = lax.broadcasted_iota(I32, (ts, hd), 1) // dh
        out = jnp.zeros((ts, hd), F32)
        for h in range(n_heads):
            out = jnp.where(ll == h, o[h * ts:(h + 1) * ts, :], out)
        o_ref[...] = out.astype(BF16)


def _attn_sample_call(q, cache_kt, cache_vt, k_new, v_new, cq_rep, ck_chunks, row0, ts):
    n_streams, n_heads, dh, past = cache_kt.shape
    hd = n_heads * dh
    ck_len = min(CACHE_CHUNK, past)
    nchunk = past // ck_len
    r = n_heads * ts
    b0 = row0 // ts
    cache_spec = pl.BlockSpec((None, n_heads, dh, ck_len), lambda b, j: (b, 0, 0, jnp.minimum(j, nchunk - 1)))
    new_spec = pl.BlockSpec((ts, hd), lambda b, j: (b0 + b, 0))
    return pl.pallas_call(
        functools.partial(_attn_sample_kernel, ts=ts, n_heads=n_heads, dh=dh, nchunk=nchunk),
        grid=(n_streams, nchunk + 1),
        in_specs=[new_spec, cache_spec, cache_spec, new_spec, new_spec,
                  pl.BlockSpec((r, LANES), lambda b, j: (b, 0)),
                  pl.BlockSpec((n_heads, ck_len), lambda b, j: (b * (nchunk + 1) + j, 0))],
        out_specs=pl.BlockSpec((ts, hd), lambda b, j: (b, 0)),
        out_shape=jax.ShapeDtypeStruct((n_streams * ts, hd), BF16),
        scratch_shapes=[pltpu.VMEM((r, hd), BF16), pltpu.VMEM((r, 1), F32), pltpu.VMEM((r, 1), F32),
                        pltpu.VMEM((r, hd), F32)],
        compiler_params=_cparams(2),
        name="attn_sample",
    )(q, cache_kt, cache_vt, k_new, v_new, cq_rep, ck_chunks)


def _wo_kernel(h_ref, op_ref, os_ref, wo_ref, nf_ref, wr_ref, brt_ref, h3_ref, slab_ref, meta_ref, cnt_ref,
               carry_ref, *, tm, d, p_tiles, n_groups, epg):
    i = pl.program_id(0)

    @pl.when(i == 0)
    def _():
        carry_ref[...] = jnp.zeros(carry_ref.shape, F32)

    o = jnp.where(i >= p_tiles, os_ref[...], op_ref[...])
    h3 = _tok_load(h_ref, tm, d) + _dot(o, wo_ref[...])
    _tok_store(h3_ref, h3)
    slab_ref[...], meta_ref[...] = _route_tail(h3, nf_ref[...], wr_ref, brt_ref[...], carry_ref, n_groups, epg)
    cnt_ref[...] = carry_ref[...]


def _wo_call(h, o_p, o_s, wo, nf, wr, brt, n_groups, epg):
    hd, d = wo.shape
    n = h.shape[0] // (d // LANES)
    tm = TOKEN_TILE
    p_tiles = o_p.shape[0] // tm
    res = _tok_spec(tm, d, lambda i: (i, 0))
    return pl.pallas_call(
        functools.partial(_wo_kernel, tm=tm, d=d, p_tiles=p_tiles, n_groups=n_groups, epg=epg),
        grid=(n // tm,),
        in_specs=[res,
                  pl.BlockSpec((tm, hd), lambda i: (jnp.minimum(i, p_tiles - 1), 0)),
                  pl.BlockSpec((tm, hd), lambda i: (jnp.maximum(i - p_tiles, 0), 0)),
                  _const_spec(wo.shape), _const_spec(nf.shape), _const_spec(wr.shape), _const_spec(brt.shape)],
        out_specs=[res, pl.BlockSpec((tm, LANES), lambda i: (i, 0)),
                   pl.BlockSpec((SUBLANES, tm), lambda i: (i, 0)), pl.BlockSpec((1, LANES), lambda i: (0, 0))],
        out_shape=[jax.ShapeDtypeStruct(h.shape, F32), jax.ShapeDtypeStruct((n, LANES), F32),
                   jax.ShapeDtypeStruct((n // tm * SUBLANES, tm), F32), jax.ShapeDtypeStruct((1, LANES), F32)],
        scratch_shapes=[pltpu.VMEM((1, LANES), F32)],
        compiler_params=_cparams(1),
        name="wo_router",
    )(h, o_p, o_s, wo, nf, wr, brt)


def _router_weights(w_group, b_group, w_router, b_router):
    d = w_group.shape[0]
    n = w_group.shape[1] + w_router.shape[1]
    w = jnp.zeros((d, LANES), F32).at[:, :n].set(jnp.concatenate([w_group, w_router], axis=1))
    w1 = w.astype(BF16)
    w2 = (w - w1.astype(F32)).astype(BF16)
    b = jnp.zeros((1, LANES), F32).at[0, :n].set(jnp.concatenate([b_group, b_router]))
    return jnp.stack([w1, w2]), b


def _moe(h, meta, counts, nf, wg, wu, wd, layer, n_groups, n_experts):
    tile = ROUTE_TILE
    c = wg.shape[-2] // LANES
    n = h.shape[0] // c
    n_pairs = 2 * n
    cnt = counts[0, n_groups:n_groups + n_experts].astype(I32)
    padded = ((cnt + tile - 1) // tile) * tile
    ends = jnp.cumsum(padded)
    offs = ends - padded
    n_tiles = (n_pairs + n_experts * (tile - 1) + tile - 1) // tile
    n_valid = (ends[-1] // tile).astype(I32)
    starts = jnp.arange(n_tiles, dtype=I32) * tile
    te = jnp.zeros((n_tiles,), I32)
    for e in range(n_experts - 1):
        te = te + (starts >= ends[e]).astype(I32)
    te = jnp.where(jnp.arange(n_tiles) < n_valid, te, te[jnp.maximum(n_valid - 1, 0)])
    m = meta.reshape(-1, SUBLANES, meta.shape[1]).astype(I32)
    ids = m[:, 0:2, :].transpose(1, 0, 2).reshape(2, n)
    ranks = m[:, 4:6, :].transpose(1, 0, 2).reshape(2, n)
    base = jnp.zeros((2, n), I32)
    for e in range(n_experts):
        base = jnp.where(ids == e, offs[e], base)
    pos = (base + ranks).reshape(n_pairs)
    n_slots = (n_tiles + 1) * tile
    inv = _invert_call(pos, n_slots)
    filled = inv >= 0
    slot_id = jnp.arange(n_slots, dtype=I32)
    src = jnp.where(filled, jnp.where(inv >= n, inv - n, inv), 0)
    dst = jnp.where(filled, inv, n_pairs + slot_id % (2 * tile))
    dst = jnp.concatenate([n_pairs + 2 * tile + slot_id[:tile], dst[:n_tiles * tile]])
    return _experts_call(te, n_valid.reshape(1), src * c, dst * c, h, nf, wg, wu, wd, layer, n_tiles)


def kernel(x_prompt, x_sample, state_conv, cache_k, cache_v, cache_logf, norm_a, w_in_a, conv_w_a, w_out_a,
           norm_kv, w_k, w_v, w_f, b_f, norm_b, w_q_b, w_o_b, norm_ffn, w_group, b_group, w_router, b_router,
           w_gate, w_up, w_down, norm_final):
    bp, tp, d = x_prompt.shape
    bs, ts, _ = x_sample.shape
    past, n_heads, dh = cache_k.shape[1], cache_k.shape[2], cache_k.shape[3]
    hd = n_heads * dh
    n_groups = w_group.shape[-1]
    n_experts = w_gate.shape[1]
    epg = n_experts // n_groups
    np_, ns = bp * tp, bs * ts
    tm = TOKEN_TILE
    assert state_conv.shape[0] == 1 and w_q_b.shape[0] == 1 and state_conv.shape[2] == 2
    assert dh * 2 == LANES and n_heads % 2 == 0
    assert tp % tm == 0 and ns % tm == 0 and tm % ts == 0 and ts >= 2
    p_tiles = np_ // tm

    row = lambda a: a.reshape(1, -1).astype(F32)
    win = w_in_a[0].astype(BF16)
    wout = w_out_a[0].astype(BF16)
    wk, wv, wq, wo = w_k.astype(BF16), w_v.astype(BF16), w_q_b[0].astype(BF16), w_o_b[0].astype(BF16)
    wf = jnp.zeros((d, LANES), F32).at[:, :n_heads].set(w_f).astype(BF16)
    bf = jnp.zeros((1, LANES), F32).at[0, :n_heads].set(b_f)
    wr0, br0 = _router_weights(w_group[0], b_group[0], w_router[0], b_router[0])
    wr1, br1 = _router_weights(w_group[1], b_group[1], w_router[1], b_router[1])

    st = state_conv[0]
    s1 = jnp.zeros((bs, ts, d), F32).at[:, 0].set(st[:, 1]).reshape(ns, d)
    s2 = jnp.zeros((bs, ts, d), F32).at[:, 0].set(st[:, 0]).at[:, 1].set(st[:, 1]).reshape(ns, d)

    h1, slab0, meta0, tails, cus, cnt0 = _mixer_call(
        x_prompt.reshape(np_, d), x_sample.reshape(ns, d), s1, s2, tp, ts, row(norm_a[0]), win, conv_w_a[0], wout,
        row(norm_ffn[0]), wr0, br0, n_groups, epg)
    conv_prompt = tails[:p_tiles * SUBLANES].reshape(bp, tp // tm, SUBLANES, d)[:, -1, SUBLANES - 2:][None]
    conv_sample = cus.reshape(bs, ts, d)[:, ts - 2:][None]
    y2 = _moe(h1, meta0, cnt0, row(norm_ffn[0]), w_gate, w_up, w_down, 0, n_groups, n_experts)

    h2, kt_p, vt_p, k_s, v_s, kb_all, vb_all, lf_all, q_all = _kvq_call(
        h1, slab0, y2, row(norm_kv), wk, wv, w_k.T.astype(BF16), w_v.T.astype(BF16), wf, bf, row(norm_b[0]), wq,
        np_ // KVQ_TILE, tp // KVQ_TILE, float(dh) ** -0.5 * LOG2E, float(dh) ** -0.5)
    k_prompt = kt_p.reshape(bp, n_heads, dh, tp).transpose(0, 3, 1, 2)
    v_prompt = vt_p.reshape(bp, n_heads, dh, tp).transpose(0, 3, 1, 2)
    logf_prompt = lf_all[:np_, :n_heads].reshape(bp, tp, n_heads)
    logf_sample = lf_all[np_:, :n_heads].reshape(bs, ts, n_heads)

    c_p = _cumsum_call(logf_prompt.transpose(0, 2, 1).reshape(bp * n_heads, tp))
    c_slab = jnp.zeros((np_, LANES), F32).at[:, :n_heads].set(
        c_p.reshape(bp, n_heads, tp).transpose(0, 2, 1).reshape(np_, n_heads))
    o_p = _attn_prompt_call(q_all, kb_all, vb_all, c_slab, bp, tp, n_heads, dh)

    tall = past + ts
    tpad = ((tall + LANES - 1) // LANES) * LANES
    lfs = jnp.concatenate([cache_logf.astype(F32), logf_sample], axis=1).transpose(0, 2, 1)
    lfs = jnp.pad(lfs, ((0, 0), (0, 0), (0, tpad - tall))).reshape(bs * n_heads, tpad)
    c_s = _cumsum_call(lfs).reshape(bs, n_heads, tpad)
    ck_len = min(CACHE_CHUNK, past)
    nchunk = past // ck_len
    c_past = c_s[:, :, :past].reshape(bs, n_heads, nchunk, ck_len).transpose(0, 2, 1, 3)
    c_new = c_s[:, :, past:past + ts]
    c_new_pad = jnp.pad(c_new, ((0, 0), (0, 0), (0, ck_len - ts)))[:, None]
    ck_s = jnp.concatenate([c_past, c_new_pad], axis=1).reshape(bs * (nchunk + 1) * n_heads, ck_len)
    cq_s = jnp.broadcast_to(c_new.reshape(bs * n_heads * ts, 1), (bs * n_heads * ts, LANES))
    o_s = _attn_sample_call(q_all, cache_k.transpose(0, 2, 3, 1), cache_v.transpose(0, 2, 3, 1), kb_all, vb_all,
                            cq_s, ck_s, np_, ts)

    h3, slab1, meta1, cnt1 = _wo_call(h2, o_p, o_s, wo, row(norm_ffn[1]), wr1, br1, n_groups, epg)
    y2b = _moe(h3, meta1, cnt1, row(norm_ffn[1]), w_gate, w_up, w_down, 1, n_groups, n_experts)
    y_p, y_s = _final_call(h3, slab1, y2b, row(norm_final), p_tiles)

    return (y_p.reshape(bp, tp, d), y_s.reshape(bs, ts, d), conv_prompt, conv_sample,
            k_prompt, v_prompt, logf_prompt,
            k_s.reshape(bs, ts, n_heads, dh), v_s.reshape(bs, ts, n_heads, dh), logf_sample)
```

```python
import functools

import jax
import jax.numpy as jnp
from jax import lax
from jax.experimental import pallas as pl
from jax.experimental.pallas import tpu as pltpu

F32 = jnp.float32
BF16 = jnp.bfloat16
I32 = jnp.int32

RMS_EPS = 1e-6
LOG2E = 1.4426950408889634
LANES = 128
SUBLANES = 8
ROUTE_TILE = 256
TOKEN_TILE = 512
KVQ_TILE = 256
ATTN_BLOCK = 512
ATTN_STRIP = 64
CACHE_CHUNK = 1024
DMA_UNROLL = 8
VMEM_LIMIT = 56 * 1024 * 1024


def _cparams(n_axes):
    return pltpu.CompilerParams(dimension_semantics=("arbitrary",) * n_axes,
                                vmem_limit_bytes=VMEM_LIMIT)


def _rms(x, g):
    return x * lax.rsqrt(jnp.mean(x * x, axis=-1, keepdims=True) + RMS_EPS) * g


def _dot(a, b):
    return jnp.dot(a, b, preferred_element_type=F32)


def _dot_nt(a, b):
    return lax.dot_general(a, b, (((1,), (1,)), ((), ())), preferred_element_type=F32)


def _tok_load(ref, rows, d):
    c = d // LANES
    return jnp.concatenate([ref[pl.ds(j, rows, stride=c), :] for j in range(c)], axis=1)


def _tok_store(ref, val):
    rows, d = val.shape
    c = d // LANES
    for j in range(c):
        ref[pl.ds(j, rows, stride=c), :] = val[:, j * LANES:(j + 1) * LANES]


def _tok_spec(rows, d, index_map):
    return pl.BlockSpec((rows * (d // LANES), LANES), index_map)


def _const_spec(shape):
    nd = len(shape)
    return pl.BlockSpec(shape, lambda *_: (0,) * nd, pipeline_mode=pl.Buffered(1))


def _route_tail(h, g_ffn, wr_ref, b_rt, carry_ref, n_groups, epg):
    tm = h.shape[0]
    hn = _rms(h, g_ffn)
    a1 = hn.astype(BF16)
    a2 = (hn - a1.astype(F32)).astype(BF16)
    w1 = wr_ref[0]
    w2 = wr_ref[1]
    logits = _dot(a1, w1) + _dot(a1, w2) + _dot(a2, w1) + b_rt
    lane = lax.broadcasted_iota(I32, logits.shape, 1)
    lanef = lane.astype(F32)
    neg = -jnp.inf
    big = 1e9
    gl = jnp.where(lane < n_groups, logits, neg)
    gmax = jnp.max(gl, axis=-1, keepdims=True)
    g_idx = jnp.min(jnp.where(gl == gmax, lanef, big), axis=-1, keepdims=True)
    g_w = 1.0 / jnp.sum(jnp.exp(gl - gmax), axis=-1, keepdims=True)
    lo = n_groups + g_idx * epg
    el = jnp.where((lanef >= lo) & (lanef < lo + epg), logits, neg)
    m1 = jnp.max(el, axis=-1, keepdims=True)
    i1 = jnp.min(jnp.where(el == m1, lanef, big), axis=-1, keepdims=True)
    el2 = jnp.where(lanef == i1, neg, el)
    m2 = jnp.max(el2, axis=-1, keepdims=True)
    i2 = jnp.min(jnp.where(el2 == m2, lanef, big), axis=-1, keepdims=True)
    t = jnp.exp(m2 - m1)
    cw1 = g_w * (1.0 / (1.0 + t))
    cw2 = g_w * (t / (1.0 + t))
    sel1 = lanef == i1
    sel2 = lanef == i2
    oh = jnp.where(sel1 | sel2, 1.0, 0.0)
    row = lax.broadcasted_iota(I32, (tm, tm), 0)
    col = lax.broadcasted_iota(I32, (tm, tm), 1)
    tri = jnp.where(col < row, 1.0, 0.0).astype(BF16)
    carry = carry_ref[...]
    prefix = _dot(tri, oh.astype(BF16)) + carry
    rank1 = jnp.sum(jnp.where(sel1, prefix, 0.0), axis=-1, keepdims=True)
    rank2 = jnp.sum(jnp.where(sel2, prefix, 0.0), axis=-1, keepdims=True)
    carry_ref[...] = carry + jnp.sum(oh, axis=0, keepdims=True)
    slab = jnp.where(lane == 0, i1 - n_groups,
           jnp.where(lane == 1, i2 - n_groups,
           jnp.where(lane == 2, cw1,
           jnp.where(lane == 3, cw2,
           jnp.where(lane == 4, rank1,
           jnp.where(lane == 5, rank2, 0.0))))))
    return slab, jnp.transpose(slab)[0:SUBLANES, :]


def _mixer_kernel(xp_ref, xs_ref, s1_ref, s2_ref, na_ref, win_ref, cw_ref, wout_ref, nf_ref, wr_ref, brt_ref,
                  h_ref, slab_ref, meta_ref, tail_ref, cus_ref, cnt_ref, cubuf, carry_ref,
                  *, tm, d, p_tiles, tiles_per_seq, s_len, n_groups, epg):
    i = pl.program_id(0)
    sample = i >= p_tiles

    @pl.when(i == 0)
    def _():
        carry_ref[...] = jnp.zeros(carry_ref.shape, F32)

    @pl.when(sample | (i % tiles_per_seq == 0))
    def _():
        cubuf[0:SUBLANES, :] = jnp.zeros((SUBLANES, d), F32)

    x = jnp.where(sample, xs_ref[...], xp_ref[...])
    xn = _rms(x, na_ref[...]).astype(BF16)
    bcu = _dot(xn, win_ref[...])
    b_gate = bcu[:, 0:d]
    cu = bcu[:, d:2 * d] * bcu[:, 2 * d:3 * d]
    cubuf[SUBLANES:SUBLANES + tm, :] = cu
    prev1 = cubuf[SUBLANES - 1:SUBLANES - 1 + tm, :]
    prev2 = cubuf[SUBLANES - 2:SUBLANES - 2 + tm, :]
    r = lax.broadcasted_iota(I32, (tm, d), 0) % s_len
    prev1 = jnp.where(sample & (r == 0), s1_ref[...], prev1)
    prev2 = jnp.where(sample & (r < 2), s2_ref[...], prev2)
    tail = cubuf[tm:tm + SUBLANES, :]
    cubuf[0:SUBLANES, :] = tail
    tail_ref[...] = tail

    @pl.when(sample)
    def _():
        cus_ref[...] = cu

    cw = cw_ref[...]
    y = cw[0:1, :] * prev2 + cw[1:2, :] * prev1 + cw[2:3, :] * cu
    h = x + _dot((b_gate * y).astype(BF16), wout_ref[...])
    _tok_store(h_ref, h)
    slab_ref[...], meta_ref[...] = _route_tail(h, nf_ref[...], wr_ref, brt_ref[...], carry_ref, n_groups, epg)
    cnt_ref[...] = carry_ref[...]


def _mixer_call(xp, xs, s1, s2, seq_len, s_len, na, win, cw, wout, nf, wr, brt, n_groups, epg):
    np_, d = xp.shape
    ns = xs.shape[0]
    tm = TOKEN_TILE
    p_tiles, s_tiles = np_ // tm, ns // tm
    n_tiles = p_tiles + s_tiles
    n = np_ + ns
    p_spec = pl.BlockSpec((tm, d), lambda i: (jnp.minimum(i, p_tiles - 1), 0))
    s_spec = pl.BlockSpec((tm, d), lambda i: (jnp.maximum(i - p_tiles, 0), 0))
    return pl.pallas_call(
        functools.partial(_mixer_kernel, tm=tm, d=d, p_tiles=p_tiles, tiles_per_seq=seq_len // tm, s_len=s_len,
                          n_groups=n_groups, epg=epg),
        grid=(n_tiles,),
        in_specs=[p_spec, s_spec, s_spec, s_spec, _const_spec(na.shape), _const_spec(win.shape),
                  _const_spec(cw.shape), _const_spec(wout.shape), _const_spec(nf.shape), _const_spec(wr.shape),
                  _const_spec(brt.shape)],
        out_specs=[_tok_spec(tm, d, lambda i: (i, 0)), pl.BlockSpec((tm, LANES), lambda i: (i, 0)),
                   pl.BlockSpec((SUBLANES, tm), lambda i: (i, 0)),
                   pl.BlockSpec((SUBLANES, d), lambda i: (i, 0)), s_spec,
                   pl.BlockSpec((1, LANES), lambda i: (0, 0))],
        out_shape=[jax.ShapeDtypeStruct((n * (d // LANES), LANES), F32), jax.ShapeDtypeStruct((n, LANES), F32),
                   jax.ShapeDtypeStruct((n_tiles * SUBLANES, tm), F32),
                   jax.ShapeDtypeStruct((n_tiles * SUBLANES, d), F32), jax.ShapeDtypeStruct((ns, d), F32),
                   jax.ShapeDtypeStruct((1, LANES), F32)],
        scratch_shapes=[pltpu.VMEM((tm + SUBLANES, d), F32), pltpu.VMEM((1, LANES), F32)],
        compiler_params=_cparams(1),
        name="mixer_a",
    )(xp, xs, s1, s2, na, win, cw, wout, nf, wr, brt)


def _invert_kernel(pos_ref, empty_hbm, inv_ref, *, n_pairs):
    pltpu.sync_copy(empty_hbm, inv_ref)

    def fill(p, c):
        inv_ref[pos_ref[p]] = p
        return c

    lax.fori_loop(0, n_pairs, fill, 0, unroll=DMA_UNROLL)


def _invert_call(pos, n_slots):
    n_pairs = pos.shape[0]
    grid_spec = pltpu.PrefetchScalarGridSpec(
        num_scalar_prefetch=1, grid=(1,), in_specs=[pl.BlockSpec(memory_space=pl.ANY)],
        out_specs=pl.BlockSpec(memory_space=pltpu.SMEM))
    return pl.pallas_call(
        functools.partial(_invert_kernel, n_pairs=n_pairs),
        grid_spec=grid_spec,
        out_shape=jax.ShapeDtypeStruct((n_slots,), I32),
        compiler_params=_cparams(1),
        name="invert_perm",
    )(pos, jnp.full((n_slots,), -1, I32))


def _experts_kernel(te_ref, nv_ref, src_ref, dst_ref, h_hbm, nf_ref, wg_ref, wu_ref, wd_ref, y_hbm,
                    xbuf, ybuf, wg_bf, wu_bf, wd_bf, gsem, ssem, *, tr, d, n_pairs):
    i = pl.program_id(0)
    nv = nv_ref[0]
    c = d // LANES

    def gather_rows(tile, s):
        base = tile * tr
        for r in range(tr):
            row0 = pl.multiple_of(src_ref[base + r], c)
            pltpu.make_async_copy(h_hbm.at[pl.ds(row0, c)], xbuf.at[s, pl.ds(r * c, c)],
                                  gsem.at[s]).start(priority=r % 2)

    def scatter_rows(dst_tile, s):
        base = dst_tile * tr
        for r in range(tr):
            row0 = pl.multiple_of(dst_ref[base + r], c)
            pltpu.make_async_copy(ybuf.at[s, pl.ds(r * c, c)], y_hbm.at[pl.ds(row0, c)],
                                  ssem.at[s]).start(priority=r % 2)

    def wait_gather(s):
        pltpu.make_async_copy(h_hbm.at[pl.ds(0, tr * c)], xbuf.at[s], gsem.at[s]).wait()

    def wait_scatter(s):
        pltpu.make_async_copy(ybuf.at[s], y_hbm.at[pl.ds(0, tr * c)], ssem.at[s]).wait()

    def step(s):
        wait_gather(s)
        prev = te_ref[jnp.maximum(i - 1, 0)]

        @pl.when((i == 0) | (te_ref[i] != prev))
        def _():
            wg_bf[...] = wg_ref[0].astype(BF16)
            wu_bf[...] = wu_ref[0].astype(BF16)
            wd_bf[...] = wd_ref[0].astype(BF16)

        x = _rms(_tok_load(xbuf.at[s], tr, d), nf_ref[...]).astype(BF16)
        g = _dot(x, wg_bf[...])
        u = _dot(x, wu_bf[...])
        hid = (g * (1.0 / (1.0 + jnp.exp(-g))) * u).astype(BF16)
        y = _dot(hid, wd_bf[...])
        gather_rows(i + 1, 1 - s)
        scatter_rows(i, 1 - s)

        @pl.when(i >= 1)
        def _():
            wait_scatter(s)

        _tok_store(ybuf.at[s], y)

        @pl.when(i == nv - 1)
        def _():
            scatter_rows(i + 1, s)
            wait_gather(1 - s)
            wait_scatter(1 - s)
            wait_scatter(s)

    @pl.when(i < nv)
    def _():
        @pl.when(i == 0)
        def _():
            ybuf[...] = jnp.zeros(ybuf.shape, F32)
            for s in range(3):
                pad = pltpu.make_async_copy(ybuf.at[0], y_hbm.at[pl.ds((n_pairs + s * tr) * c, tr * c)],
                                            ssem.at[0])
                pad.start()
                pad.wait()
            gather_rows(0, 0)

        @pl.when(i % 2 == 0)
        def _():
            step(0)

        @pl.when(i % 2 == 1)
        def _():
            step(1)


def _experts_call(tile_expert, n_valid, src, dst, h, nf, wg, wu, wd, layer, n_tiles):
    d, f = wg.shape[-2], wg.shape[-1]
    c = d // LANES
    n = h.shape[0] // c
    tr = ROUTE_TILE
    n_pairs = 2 * n
    w_spec = lambda shape: pl.BlockSpec(shape, lambda i, te, nv, s, t: (layer, te[i], 0, 0))
    grid_spec = pltpu.PrefetchScalarGridSpec(
        num_scalar_prefetch=4,
        grid=(n_tiles,),
        in_specs=[pl.BlockSpec(memory_space=pl.ANY),
                  pl.BlockSpec((1, d), lambda i, te, nv, s, t: (0, 0)),
                  w_spec((None, 1, d, f)), w_spec((None, 1, d, f)), w_spec((None, 1, f, d))],
        out_specs=pl.BlockSpec(memory_space=pl.ANY),
        scratch_shapes=[pltpu.VMEM((2, tr * c, LANES), F32), pltpu.VMEM((2, tr * c, LANES), F32),
                        pltpu.VMEM((d, f), BF16), pltpu.VMEM((d, f), BF16), pltpu.VMEM((f, d), BF16),
                        pltpu.SemaphoreType.DMA((2,)), pltpu.SemaphoreType.DMA((2,))],
    )
    return pl.pallas_call(
        functools.partial(_experts_kernel, tr=tr, d=d, n_pairs=n_pairs),
        grid_spec=grid_spec,
        out_shape=jax.ShapeDtypeStruct(((n_pairs + 3 * tr) * c, LANES), F32),
        compiler_params=_cparams(1),
        name="moe_experts",
    )(tile_expert, n_valid, src, dst, h, nf, wg, wu, wd)


def _combine(h, slab, ya, yb):
    return h + (slab[:, 2:3] * ya + slab[:, 3:4] * yb)


def _kvq_kernel(h_ref, slab_ref, ya_ref, yb_ref, nkv_ref, wk_ref, wv_ref, wkt_ref, wvt_ref, wf_ref, bf_ref, nb_ref,
                wq_ref, h2_ref, ktp_ref, vtp_ref, ks_ref, vs_ref, kb_ref, vb_ref, lf_ref, q_ref,
                *, tm, d, p_tiles, qs_prompt, qs_sample):
    prompt = pl.program_id(0) < p_tiles
    h2 = _combine(_tok_load(h_ref, tm, d), slab_ref[...], _tok_load(ya_ref, tm, d), _tok_load(yb_ref, tm, d))
    _tok_store(h2_ref, h2)
    s = _rms(h2, nkv_ref[...]).astype(BF16)
    k = _dot(s, wk_ref[...])
    v = _dot(s, wv_ref[...])

    @pl.when(prompt)
    def _():
        ktp_ref[...] = _dot_nt(wkt_ref[...], s)
        vtp_ref[...] = _dot_nt(wvt_ref[...], s)

    @pl.when(jnp.logical_not(prompt))
    def _():
        ks_ref[...] = k
        vs_ref[...] = v

    kb_ref[...] = k.astype(BF16)
    vb_ref[...] = v.astype(BF16)
    z = _dot(s, wf_ref[...]) + bf_ref[...]
    lf_ref[...] = -(jnp.maximum(-z, 0.0) + jnp.log1p(jnp.exp(-jnp.abs(z))))
    qn = _rms(h2, nb_ref[...]).astype(BF16)
    q_scale = jnp.where(prompt, qs_prompt, qs_sample)
    q_ref[...] = (_dot(qn, wq_ref[...]) * q_scale).astype(BF16)


def _kvq_call(h, slab, y2, nkv, wk, wv, wkt, wvt, wf, bf, nb, wq, p_tiles, tiles_per_seq, qs_prompt, qs_sample):
    d, hd = wk.shape
    n = h.shape[0] // (d // LANES)
    tm = KVQ_TILE
    nt = n // tm
    np_ = p_tiles * tm
    tok = lambda w: pl.BlockSpec((tm, w), lambda i: (i, 0))
    res = _tok_spec(tm, d, lambda i: (i, 0))

    def p_map(i):
        ip = jnp.minimum(i, p_tiles - 1)
        return (ip // tiles_per_seq, 0, ip % tiles_per_seq)

    p_out = pl.BlockSpec((None, hd, tm), p_map)
    s_out = pl.BlockSpec((tm, hd), lambda i: (jnp.maximum(i - p_tiles, 0), 0))
    kt_shape = jax.ShapeDtypeStruct((p_tiles // tiles_per_seq, hd, tiles_per_seq * tm), F32)
    return pl.pallas_call(
        functools.partial(_kvq_kernel, tm=tm, d=d, p_tiles=p_tiles, qs_prompt=qs_prompt, qs_sample=qs_sample),
        grid=(nt,),
        in_specs=[res, tok(LANES), res, _tok_spec(tm, d, lambda i: (i + nt, 0)),
                  _const_spec(nkv.shape), _const_spec(wk.shape), _const_spec(wv.shape), _const_spec(wkt.shape),
                  _const_spec(wvt.shape), _const_spec(wf.shape), _const_spec(bf.shape), _const_spec(nb.shape),
                  _const_spec(wq.shape)],
        out_specs=[res, p_out, p_out, s_out, s_out, tok(hd), tok(hd), tok(LANES), tok(hd)],
        out_shape=[jax.ShapeDtypeStruct(h.shape, F32), kt_shape, kt_shape,
                   jax.ShapeDtypeStruct((n - np_, hd), F32),
                   jax.ShapeDtypeStruct((n - np_, hd), F32), jax.ShapeDtypeStruct((n, hd), BF16),
                   jax.ShapeDtypeStruct((n, hd), BF16), jax.ShapeDtypeStruct((n, LANES), F32),
                   jax.ShapeDtypeStruct((n, hd), BF16)],
        compiler_params=_cparams(1),
        name="combine_kvq",
    )(h, slab, y2, y2, nkv, wk, wv, wkt, wvt, wf, bf, nb, wq)


def _final_kernel(h_ref, slab_ref, ya_ref, yb_ref, nfin_ref, yp_ref, ys_ref, *, tm, d, p_tiles):
    h4 = _combine(_tok_load(h_ref, tm, d), slab_ref[...], _tok_load(ya_ref, tm, d), _tok_load(yb_ref, tm, d))
    y = _rms(h4, nfin_ref[...])
    prompt = pl.program_id(0) < p_tiles

    @pl.when(prompt)
    def _():
        yp_ref[...] = y

    @pl.when(jnp.logical_not(prompt))
    def _():
        ys_ref[...] = y


def _final_call(h, slab, y2, nfin, p_tiles):
    d = nfin.shape[1]
    n = h.shape[0] // (d // LANES)
    tm = TOKEN_TILE
    nt = n // tm
    np_ = p_tiles * tm
    res = _tok_spec(tm, d, lambda i: (i, 0))
    return pl.pallas_call(
        functools.partial(_final_kernel, tm=tm, d=d, p_tiles=p_tiles),
        grid=(nt,),
        in_specs=[res, pl.BlockSpec((tm, LANES), lambda i: (i, 0)), res, _tok_spec(tm, d, lambda i: (i + nt, 0)),
                  _const_spec(nfin.shape)],
        out_specs=[pl.BlockSpec((tm, d), lambda i: (jnp.minimum(i, p_tiles - 1), 0)),
                   pl.BlockSpec((tm, d), lambda i: (jnp.maximum(i - p_tiles, 0), 0))],
        out_shape=[jax.ShapeDtypeStruct((np_, d), F32), jax.ShapeDtypeStruct((n - np_, d), F32)],
        compiler_params=_cparams(1),
        name="combine_final",
    )(h, slab, y2, y2, nfin)


def _cumsum_kernel(x_ref, c_ref, *, t):
    rb = x_ref.shape[0]
    row = lax.broadcasted_iota(I32, (LANES, LANES), 0)
    col = lax.broadcasted_iota(I32, (LANES, LANES), 1)
    tri = jnp.where(row <= col, 1.0, 0.0).astype(BF16)
    carry = jnp.zeros((rb, 1), F32)
    for j in range(t // LANES):
        blk = x_ref[:, j * LANES:(j + 1) * LANES]
        a1 = blk.astype(BF16)
        r1 = blk - a1.astype(F32)
        a2 = r1.astype(BF16)
        a3 = (r1 - a2.astype(F32)).astype(BF16)
        cs = _dot(a1, tri) + _dot(a2, tri) + _dot(a3, tri) + carry
        c_ref[:, j * LANES:(j + 1) * LANES] = cs
        carry = cs[:, LANES - 1:LANES]


def _cumsum_call(x):
    r, t = x.shape
    rb = min(r, 128)
    return pl.pallas_call(
        functools.partial(_cumsum_kernel, t=t),
        grid=(r // rb,),
        in_specs=[pl.BlockSpec((rb, t), lambda i: (i, 0))],
        out_specs=pl.BlockSpec((rb, t), lambda i: (i, 0)),
        out_shape=jax.ShapeDtypeStruct((r, t), F32),
        compiler_params=_cparams(1),
        name="logf_cumsum",
    )(x)


def _bias_columns(c2, base, lane, query_side):
    p1 = c2.astype(BF16).astype(F32)
    r1 = c2 - p1
    p2 = r1.astype(BF16).astype(F32)
    p3 = r1 - p2
    off = lane - base
    if query_side:
        ones = (off >= 3) & (off < 6)
        return jnp.where(off == 0, p1, jnp.where(off == 1, p2, jnp.where(off == 2, p3, jnp.where(ones, 1.0, 0.0))))
    ones = (off >= 0) & (off < 3)
    return jnp.where(off == 3, -p1, jnp.where(off == 4, -p2, jnp.where(off == 5, -p3, jnp.where(ones, 1.0, 0.0))))


def _attn_prompt_kernel(q_ref, k_ref, v_ref, c_ref, o_ref, kx0, kx1, vx0, vx1, qx_buf, s_buf, p_buf, m_buf, acc_buf,
                        *, blk, dh, nkv, strip):
    hp = pl.program_id(1)
    qi = pl.program_id(2)
    kx = (kx0, kx1)
    vx = (vx0, vx1)
    lane = lax.broadcasted_iota(I32, (blk, 2 * dh), 1)
    in_head = [(lane >= hd_ * dh) & (lane < (hd_ + 1) * dh) for hd_ in range(2)]
    ext_base = [dh, 0]

    def column(cs, h):
        return jnp.sum(jnp.where(lane == h, cs, 0.0), axis=-1, keepdims=True) * LOG2E

    @pl.when(qi == 0)
    def _():
        for jc in range(nkv):
            rows = slice(jc * blk, (jc + 1) * blk)
            kb = k_ref[rows, :]
            vb = v_ref[rows, :]
            cs = c_ref[rows, :]
            for hd_ in range(2):
                ext = _bias_columns(column(cs, 2 * hp + hd_), ext_base[hd_], lane, False)
                kx[hd_][rows, :] = jnp.where(in_head[hd_], kb, ext.astype(BF16))
                vx[hd_][rows, :] = jnp.where(in_head[hd_], vb, jnp.ones_like(vb))

    q = q_ref[...]
    cs_q = c_ref[pl.ds(pl.multiple_of(qi * blk, blk), blk), :]
    for hd_ in range(2):
        ext = _bias_columns(column(cs_q, 2 * hp + hd_), ext_base[hd_], lane, True)
        qx_buf[hd_] = jnp.where(in_head[hd_], q, ext.astype(BF16))
        m_buf[hd_] = jnp.full((blk, 1), -jnp.inf, F32)
        acc_buf[hd_] = jnp.zeros((blk, 2 * dh), F32)

    def scores(j):
        start = pl.multiple_of(j * blk, blk)
        for hd_ in range(2):
            s_buf[hd_] = _dot_nt(qx_buf[hd_], kx[hd_][pl.ds(start, blk), :])

    def softmax(masked):
        for hd_ in range(2):
            for r0 in range(0, blk, strip):
                rows = slice(r0, r0 + strip)
                s = s_buf[hd_, rows, :]
                if masked:
                    row = lax.broadcasted_iota(I32, (strip, blk), 0) + r0
                    col = lax.broadcasted_iota(I32, (strip, blk), 1)
                    s = jnp.where(col <= row, s, -jnp.inf)
                m_old = m_buf[hd_, rows, :]
                m_new = jnp.maximum(m_old, jnp.max(s, axis=-1, keepdims=True))
                p_buf[hd_, rows, :] = jnp.exp2(s - m_new).astype(BF16)
                m_buf[hd_, rows, :] = m_new
                acc_buf[hd_, rows, :] = jnp.exp2(m_old - m_new) * acc_buf[hd_, rows, :]

    def values(j):
        start = pl.multiple_of(j * blk, blk)
        for hd_ in range(2):
            acc_buf[hd_] += _dot(p_buf[hd_], vx[hd_][pl.ds(start, blk), :])

    def body(j, carry):
        softmax(False)
        scores(j + 1)
        values(j)
        return carry

    scores(0)
    lax.fori_loop(0, qi, body, 0)
    softmax(True)
    values(qi)
    acc0 = acc_buf[0]
    acc1 = acc_buf[1]
    o0 = acc0 / pltpu.roll(acc0, dh, 1)
    o1 = acc1 / pltpu.roll(acc1, dh, 1)
    o_ref[...] = jnp.where(in_head[0], o0, o1).astype(BF16)


def _attn_prompt_call(q, kb, vb, c_slab, batch, seq, n_heads, dh):
    blk = min(ATTN_BLOCK, seq)
    nq = seq // blk
    hd = n_heads * dh
    pair = pl.BlockSpec((seq, 2 * dh), lambda b, hp, qi: (b, hp))
    return pl.pallas_call(
        functools.partial(_attn_prompt_kernel, blk=blk, dh=dh, nkv=nq, strip=min(ATTN_STRIP, blk)),
        grid=(batch, n_heads // 2, nq),
        in_specs=[pl.BlockSpec((blk, 2 * dh), lambda b, hp, qi: (b * nq + qi, hp)), pair, pair,
                  pl.BlockSpec((seq, LANES), lambda b, hp, qi: (b, 0))],
        out_specs=pl.BlockSpec((blk, 2 * dh), lambda b, hp, qi: (b * nq + qi, hp)),
        out_shape=jax.ShapeDtypeStruct((batch * seq, hd), BF16),
        scratch_shapes=[pltpu.VMEM((seq, 2 * dh), BF16)] * 4 + [
            pltpu.VMEM((2, blk, 2 * dh), BF16), pltpu.VMEM((2, blk, blk), F32), pltpu.VMEM((2, blk, blk), BF16),
            pltpu.VMEM((2, blk, 1), F32), pltpu.VMEM((2, blk, 2 * dh), F32)],
        compiler_params=_cparams(3),
        name="attn_prompt",
    )(q, kb, vb, c_slab)


def _attn_sample_kernel(q_ref, kc_ref, vc_ref, kn_ref, vn_ref, cq_ref, ck_ref, o_ref,
                        qbd, m_ref, l_ref, acc_ref, *, ts, n_heads, dh, nchunk):
    j = pl.program_id(1)
    r = n_heads * ts
    hd = n_heads * dh

    @pl.when(j == 0)
    def _():
        q = q_ref[...]
        qt = jnp.concatenate([q] * n_heads, axis=0)
        rr = lax.broadcasted_iota(I32, (r, hd), 0) // ts
        ll = lax.broadcasted_iota(I32, (r, hd), 1) // dh
        qbd[...] = jnp.where(rr == ll, qt, jnp.zeros_like(qt))
        m_ref[...] = jnp.full((r, 1), -jnp.inf, F32)
        l_ref[...] = jnp.zeros((r, 1), F32)
        acc_ref[...] = jnp.zeros((r, hd), F32)

    def expand(ck):
        w = ck.shape[1]
        return jnp.concatenate([jnp.broadcast_to(ck[h:h + 1, :], (ts, w)) for h in range(n_heads)], axis=0)

    def update(s, pv):
        m = m_ref[...]
        m_new = jnp.maximum(m, jnp.max(s, axis=-1, keepdims=True))
        p = jnp.exp(s - m_new)
        alpha = jnp.exp(m - m_new)
        l_ref[...] = alpha * l_ref[...] + jnp.sum(p, axis=-1, keepdims=True)
        acc_ref[...] = alpha * acc_ref[...] + pv(p.astype(BF16))
        m_ref[...] = m_new

    cq = cq_ref[...][:, 0:1]

    @pl.when(j < nchunk)
    def _():
        w = kc_ref.shape[-1]
        kt = kc_ref[...].reshape(hd, w).astype(BF16)
        vt = vc_ref[...].reshape(hd, w).astype(BF16)
        s = _dot(qbd[...], kt) + (cq - expand(ck_ref[...]))
        update(s, lambda p: _dot_nt(p, vt))

    @pl.when(j == nchunk)
    def _():
        s = _dot_nt(qbd[...], kn_ref[...]) + (cq - expand(ck_ref[:, 0:ts]))
        qpos = lax.broadcasted_iota(I32, (r, ts), 0) % ts
        kpos = lax.broadcasted_iota(I32, (r, ts), 1)
        s = jnp.where(kpos <= qpos, s, -jnp.inf)
        update(s, lambda p: _dot(p, vn_ref[...]))
        o = acc_ref[...] / l_ref[...]
        ll = lax.broadcasted_iota(I32, (ts, hd), 1) // dh
        out = jnp.zeros((ts, hd), F32)
        for h in range(n_heads):
            out = jnp.where(ll == h, o[h * ts:(h + 1) * ts, :], out)
        o_ref[...] = out.astype(BF16)


def _attn_sample_call(q, cache_kt, cache_vt, k_new, v_new, cq_rep, ck_chunks, row0, ts):
    n_streams, n_heads, dh, past = cache_kt.shape
    hd = n_heads * dh
    ck_len = min(CACHE_CHUNK, past)
    nchunk = past // ck_len
    r = n_heads * ts
    b0 = row0 // ts
    cache_spec = pl.BlockSpec((None, n_heads, dh, ck_len), lambda b, j: (b, 0, 0, jnp.minimum(j, nchunk - 1)))
    new_spec = pl.BlockSpec((ts, hd), lambda b, j: (b0 + b, 0))
    return pl.pallas_call(
        functools.partial(_attn_sample_kernel, ts=ts, n_heads=n_heads, dh=dh, nchunk=nchunk),
        grid=(n_streams, nchunk + 1),
        in_specs=[new_spec, cache_spec, cache_spec, new_spec, new_spec,
                  pl.BlockSpec((r, LANES), lambda b, j: (b, 0)),
                  pl.BlockSpec((n_heads, ck_len), lambda b, j: (b * (nchunk + 1) + j, 0))],
        out_specs=pl.BlockSpec((ts, hd), lambda b, j: (b, 0)),
        out_shape=jax.ShapeDtypeStruct((n_streams * ts, hd), BF16),
        scratch_shapes=[pltpu.VMEM((r, hd), BF16), pltpu.VMEM((r, 1), F32), pltpu.VMEM((r, 1), F32),
                        pltpu.VMEM((r, hd), F32)],
        compiler_params=_cparams(2),
        name="attn_sample",
    )(q, cache_kt, cache_vt, k_new, v_new, cq_rep, ck_chunks)


def _wo_kernel(h_ref, op_ref, os_ref, wo_ref, nf_ref, wr_ref, brt_ref, h3_ref, slab_ref, meta_ref, cnt_ref,
               carry_ref, *, tm, d, p_tiles, n_groups, epg):
    i = pl.program_id(0)

    @pl.when(i == 0)
    def _():
        carry_ref[...] = jnp.zeros(carry_ref.shape, F32)

    o = jnp.where(i >= p_tiles, os_ref[...], op_ref[...])
    h3 = _tok_load(h_ref, tm, d) + _dot(o, wo_ref[...])
    _tok_store(h3_ref, h3)
    slab_ref[...], meta_ref[...] = _route_tail(h3, nf_ref[...], wr_ref, brt_ref[...], carry_ref, n_groups, epg)
    cnt_ref[...] = carry_ref[...]


def _wo_call(h, o_p, o_s, wo, nf, wr, brt, n_groups, epg):
    hd, d = wo.shape
    n = h.shape[0] // (d // LANES)
    tm = TOKEN_TILE
    p_tiles = o_p.shape[0] // tm
    res = _tok_spec(tm, d, lambda i: (i, 0))
    return pl.pallas_call(
        functools.partial(_wo_kernel, tm=tm, d=d, p_tiles=p_tiles, n_groups=n_groups, epg=epg),
        grid=(n // tm,),
        in_specs=[res,
                  pl.BlockSpec((tm, hd), lambda i: (jnp.minimum(i, p_tiles - 1), 0)),
                  pl.BlockSpec((tm, hd), lambda i: (jnp.maximum(i - p_tiles, 0), 0)),
                  _const_spec(wo.shape), _const_spec(nf.shape), _const_spec(wr.shape), _const_spec(brt.shape)],
        out_specs=[res, pl.BlockSpec((tm, LANES), lambda i: (i, 0)),
                   pl.BlockSpec((SUBLANES, tm), lambda i: (i, 0)), pl.BlockSpec((1, LANES), lambda i: (0, 0))],
        out_shape=[jax.ShapeDtypeStruct(h.shape, F32), jax.ShapeDtypeStruct((n, LANES), F32),
                   jax.ShapeDtypeStruct((n // tm * SUBLANES, tm), F32), jax.ShapeDtypeStruct((1, LANES), F32)],
        scratch_shapes=[pltpu.VMEM((1, LANES), F32)],
        compiler_params=_cparams(1),
        name="wo_router",
    )(h, o_p, o_s, wo, nf, wr, brt)


def _router_weights(w_group, b_group, w_router, b_router):
    d = w_group.shape[0]
    n = w_group.shape[1] + w_router.shape[1]
    w = jnp.zeros((d, LANES), F32).at[:, :n].set(jnp.concatenate([w_group, w_router], axis=1))
    w1 = w.astype(BF16)
    w2 = (w - w1.astype(F32)).astype(BF16)
    b = jnp.zeros((1, LANES), F32).at[0, :n].set(jnp.concatenate([b_group, b_router]))
    return jnp.stack([w1, w2]), b


def _moe(h, meta, counts, nf, wg, wu, wd, layer, n_groups, n_experts):
    tile = ROUTE_TILE
    c = wg.shape[-2] // LANES
    n = h.shape[0] // c
    n_pairs = 2 * n
    cnt = counts[0, n_groups:n_groups + n_experts].astype(I32)
    padded = ((cnt + tile - 1) // tile) * tile
    ends = jnp.cumsum(padded)
    offs = ends - padded
    n_tiles = (n_pairs + n_experts * (tile - 1) + tile - 1) // tile
    n_valid = (ends[-1] // tile).astype(I32)
    starts = jnp.arange(n_tiles, dtype=I32) * tile
    te = jnp.zeros((n_tiles,), I32)
    for e in range(n_experts - 1):
        te = te + (starts >= ends[e]).astype(I32)
    te = jnp.where(jnp.arange(n_tiles) < n_valid, te, te[jnp.maximum(n_valid - 1, 0)])
    m = meta.reshape(-1, SUBLANES, meta.shape[1]).astype(I32)
    ids = m[:, 0:2, :].transpose(1, 0, 2).reshape(2, n)
    ranks = m[:, 4:6, :].transpose(1, 0, 2).reshape(2, n)
    base = jnp.zeros((2, n), I32)
    for e in range(n_experts):
        base = jnp.where(ids == e, offs[e], base)
    pos = (base + ranks).reshape(n_pairs)
    n_slots = (n_tiles + 1) * tile
    inv = _invert_call(pos, n_slots)
    filled = inv >= 0
    slot_id = jnp.arange(n_slots, dtype=I32)
    src = jnp.where(filled, jnp.where(inv >= n, inv - n, inv), 0)
    dst = jnp.where(filled, inv, n_pairs + slot_id % (2 * tile))
    dst = jnp.concatenate([n_pairs + 2 * tile + slot_id[:tile], dst[:n_tiles * tile]])
    return _experts_call(te, n_valid.reshape(1), src * c, dst * c, h, nf, wg, wu, wd, layer, n_tiles)


def kernel(x_prompt, x_sample, state_conv, cache_k, cache_v, cache_logf, norm_a, w_in_a, conv_w_a, w_out_a,
           norm_kv, w_k, w_v, w_f, b_f, norm_b, w_q_b, w_o_b, norm_ffn, w_group, b_group, w_router, b_router,
           w_gate, w_up, w_down, norm_final):
    bp, tp, d = x_prompt.shape
    bs, ts, _ = x_sample.shape
    past, n_heads, dh = cache_k.shape[1], cache_k.shape[2], cache_k.shape[3]
    hd = n_heads * dh
    n_groups = w_group.shape[-1]
    n_experts = w_gate.shape[1]
    epg = n_experts // n_groups
    np_, ns = bp * tp, bs * ts
    tm = TOKEN_TILE
    assert state_conv.shape[0] == 1 and w_q_b.shape[0] == 1 and state_conv.shape[2] == 2
    assert dh * 2 == LANES and n_heads % 2 == 0
    assert tp % tm == 0 and ns % tm == 0 and tm % ts == 0 and ts >= 2
    p_tiles = np_ // tm

    row = lambda a: a.reshape(1, -1).astype(F32)
    win = w_in_a[0].astype(BF16)
    wout = w_out_a[0].astype(BF16)
    wk, wv, wq, wo = w_k.astype(BF16), w_v.astype(BF16), w_q_b[0].astype(BF16), w_o_b[0].astype(BF16)
    wf = jnp.zeros((d, LANES), F32).at[:, :n_heads].set(w_f).astype(BF16)
    bf = jnp.zeros((1, LANES), F32).at[0, :n_heads].set(b_f)
    wr0, br0 = _router_weights(w_group[0], b_group[0], w_router[0], b_router[0])
    wr1, br1 = _router_weights(w_group[1], b_group[1], w_router[1], b_router[1])

    st = state_conv[0]
    s1 = jnp.zeros((bs, ts, d), F32).at[:, 0].set(st[:, 1]).reshape(ns, d)
    s2 = jnp.zeros((bs, ts, d), F32).at[:, 0].set(st[:, 0]).at[:, 1].set(st[:, 1]).reshape(ns, d)

    h1, slab0, meta0, tails, cus, cnt0 = _mixer_call(
        x_prompt.reshape(np_, d), x_sample.reshape(ns, d), s1, s2, tp, ts, row(norm_a[0]), win, conv_w_a[0], wout,
        row(norm_ffn[0]), wr0, br0, n_groups, epg)
    conv_prompt = tails[:p_tiles * SUBLANES].reshape(bp, tp // tm, SUBLANES, d)[:, -1, SUBLANES - 2:][None]
    conv_sample = cus.reshape(bs, ts, d)[:, ts - 2:][None]
    y2 = _moe(h1, meta0, cnt0, row(norm_ffn[0]), w_gate, w_up, w_down, 0, n_groups, n_experts)

    h2, kt_p, vt_p, k_s, v_s, kb_all, vb_all, lf_all, q_all = _kvq_call(
        h1, slab0, y2, row(norm_kv), wk, wv, w_k.T.astype(BF16), w_v.T.astype(BF16), wf, bf, row(norm_b[0]), wq,
        np_ // KVQ_TILE, tp // KVQ_TILE, float(dh) ** -0.5 * LOG2E, float(dh) ** -0.5)
    k_prompt = kt_p.reshape(bp, n_heads, dh, tp).transpose(0, 3, 1, 2)
    v_prompt = vt_p.reshape(bp, n_heads, dh, tp).transpose(0, 3, 1, 2)
    logf_prompt = lf_all[:np_, :n_heads].reshape(bp, tp, n_heads)
    logf_sample = lf_all[np_:, :n_heads].reshape(bs, ts, n_heads)

    c_p = _cumsum_call(logf_prompt.transpose(0, 2, 1).reshape(bp * n_heads, tp))
    c_slab = jnp.zeros((np_, LANES), F32).at[:, :n_heads].set(
        c_p.reshape(bp, n_heads, tp).transpose(0, 2, 1).reshape(np_, n_heads))
    o_p = _attn_prompt_call(q_all, kb_all, vb_all, c_slab, bp, tp, n_heads, dh)

    tall = past + ts
    tpad = ((tall + LANES - 1) // LANES) * LANES
    lfs = jnp.concatenate([cache_logf.astype(F32), logf_sample], axis=1).transpose(0, 2, 1)
    lfs = jnp.pad(lfs, ((0, 0), (0, 0), (0, tpad - tall))).reshape(bs * n_heads, tpad)
    c_s = _cumsum_call(lfs).reshape(bs, n_heads, tpad)
    ck_len = min(CACHE_CHUNK, past)
    nchunk = past // ck_len
    c_past = c_s[:, :, :past].reshape(bs, n_heads, nchunk, ck_len).transpose(0, 2, 1, 3)
    c_new = c_s[:, :, past:past + ts]
    c_new_pad = jnp.pad(c_new, ((0, 0), (0, 0), (0, ck_len - ts)))[:, None]
    ck_s = jnp.concatenate([c_past, c_new_pad], axis=1).reshape(bs * (nchunk + 1) * n_heads, ck_len)
    cq_s = jnp.broadcast_to(c_new.reshape(bs * n_heads * ts, 1), (bs * n_heads * ts, LANES))
    o_s = _attn_sample_call(q_all, cache_k.transpose(0, 2, 3, 1), cache_v.transpose(0, 2, 3, 1), kb_all, vb_all,
                            cq_s, ck_s, np_, ts)

    h3, slab1, meta1, cnt1 = _wo_call(h2, o_p, o_s, wo, row(norm_ffn[1]), wr1, br1, n_groups, epg)
    y2b = _moe(h3, meta1, cnt1, row(norm_ffn[1]), w_gate, w_up, w_down, 1, n_groups, n_experts)
    y_p, y_s = _final_call(h3, slab1, y2b, row(norm_final), p_tiles)

    return (y_p.reshape(bp, tp, d), y_s.reshape(bs, ts, d), conv_prompt, conv_sample,
            k_prompt, v_prompt, logf_prompt,
            k_s.reshape(bs, ts, n_heads, dh), v_s.reshape(bs, ts, n_heads, dh), logf_sample)
```

```python
import functools

import jax
import jax.numpy as jnp
from jax import lax
from jax.experimental import pallas as pl
from jax.experimental.pallas import tpu as pltpu

F32 = jnp.float32
BF16 = jnp.bfloat16
I32 = jnp.int32

RMS_EPS = 1e-6
LOG2E = 1.4426950408889634
LANES = 128
SUBLANES = 8
ROUTE_TILE = 256
TOKEN_TILE = 512
KVQ_TILE = 256
ATTN_BLOCK = 512
ATTN_STRIP = 64
CACHE_CHUNK = 1024
DMA_UNROLL = 8
VMEM_LIMIT = 56 * 1024 * 1024


def _cparams(n_axes):
    return pltpu.CompilerParams(dimension_semantics=("arbitrary",) * n_axes,
                                vmem_limit_bytes=VMEM_LIMIT)


def _rms(x, g):
    return x * lax.rsqrt(jnp.mean(x * x, axis=-1, keepdims=True) + RMS_EPS) * g


def _dot(a, b):
    return jnp.dot(a, b, preferred_element_type=F32)


def _dot_nt(a, b):
    return lax.dot_general(a, b, (((1,), (1,)), ((), ())), preferred_element_type=F32)


def _tok_load(ref, rows, d):
    c = d // LANES
    return jnp.concatenate([ref[pl.ds(j, rows, stride=c), :] for j in range(c)], axis=1)


def _tok_store(ref, val):
    rows, d = val.shape
    c = d // LANES
    for j in range(c):
        ref[pl.ds(j, rows, stride=c), :] = val[:, j * LANES:(j + 1) * LANES]


def _tok_spec(rows, d, index_map):
    return pl.BlockSpec((rows * (d // LANES), LANES), index_map)


def _const_spec(shape):
    nd = len(shape)
    return pl.BlockSpec(shape, lambda *_: (0,) * nd, pipeline_mode=pl.Buffered(1))


def _route_tail(h, g_ffn, wr_ref, b_rt, carry_ref, n_groups, epg):
    tm = h.shape[0]
    hn = _rms(h, g_ffn)
    a1 = hn.astype(BF16)
    a2 = (hn - a1.astype(F32)).astype(BF16)
    w1 = wr_ref[0]
    w2 = wr_ref[1]
    logits = _dot(a1, w1) + _dot(a1, w2) + _dot(a2, w1) + b_rt
    lane = lax.broadcasted_iota(I32, logits.shape, 1)
    lanef = lane.astype(F32)
    neg = -jnp.inf
    big = 1e9
    gl = jnp.where(lane < n_groups, logits, neg)
    gmax = jnp.max(gl, axis=-1, keepdims=True)
    g_idx = jnp.min(jnp.where(gl == gmax, lanef, big), axis=-1, keepdims=True)
    g_w = 1.0 / jnp.sum(jnp.exp(gl - gmax), axis=-1, keepdims=True)
    lo = n_groups + g_idx * epg
    el = jnp.where((lanef >= lo) & (lanef < lo + epg), logits, neg)
    m1 = jnp.max(el, axis=-1, keepdims=True)
    i1 = jnp.min(jnp.where(el == m1, lanef, big), axis=-1, keepdims=True)
    el2 = jnp.where(lanef == i1, neg, el)
    m2 = jnp.max(el2, axis=-1, keepdims=True)
    i2 = jnp.min(jnp.where(el2 == m2, lanef, big), axis=-1, keepdims=True)
    t = jnp.exp(m2 - m1)
    cw1 = g_w * (1.0 / (1.0 + t))
    cw2 = g_w * (t / (1.0 + t))
    sel1 = lanef == i1
    sel2 = lanef == i2
    oh = jnp.where(sel1 | sel2, 1.0, 0.0)
    row = lax.broadcasted_iota(I32, (tm, tm), 0)
    col = lax.broadcasted_iota(I32, (tm, tm), 1)
    tri = jnp.where(col < row, 1.0, 0.0).astype(BF16)
    carry = carry_ref[...]
    prefix = _dot(tri, oh.astype(BF16)) + carry
    rank1 = jnp.sum(jnp.where(sel1, prefix, 0.0), axis=-1, keepdims=True)
    rank2 = jnp.sum(jnp.where(sel2, prefix, 0.0), axis=-1, keepdims=True)
    carry_ref[...] = carry + jnp.sum(oh, axis=0, keepdims=True)
    slab = jnp.where(lane == 0, i1 - n_groups,
           jnp.where(lane == 1, i2 - n_groups,
           jnp.where(lane == 2, cw1,
           jnp.where(lane == 3, cw2,
           jnp.where(lane == 4, rank1,
           jnp.where(lane == 5, rank2, 0.0))))))
    return slab, jnp.transpose(slab)[0:SUBLANES, :]


def _mixer_kernel(xp_ref, xs_ref, s1_ref, s2_ref, na_ref, win_ref, cw_ref, wout_ref, nf_ref, wr_ref, brt_ref,
                  h_ref, slab_ref, meta_ref, tail_ref, cus_ref, cnt_ref, cubuf, carry_ref,
                  *, tm, d, p_tiles, tiles_per_seq, s_len, n_groups, epg):
    i = pl.program_id(0)
    sample = i >= p_tiles

    @pl.when(i == 0)
    def _():
        carry_ref[...] = jnp.zeros(carry_ref.shape, F32)

    @pl.when(sample | (i % tiles_per_seq == 0))
    def _():
        cubuf[0:SUBLANES, :] = jnp.zeros((SUBLANES, d), F32)

    x = jnp.where(sample, xs_ref[...], xp_ref[...])
    xn = _rms(x, na_ref[...]).astype(BF16)
    bcu = _dot(xn, win_ref[...])
    b_gate = bcu[:, 0:d]
    cu = bcu[:, d:2 * d] * bcu[:, 2 * d:3 * d]
    cubuf[SUBLANES:SUBLANES + tm, :] = cu
    prev1 = cubuf[SUBLANES - 1:SUBLANES - 1 + tm, :]
    prev2 = cubuf[SUBLANES - 2:SUBLANES - 2 + tm, :]
    r = lax.broadcasted_iota(I32, (tm, d), 0) % s_len
    prev1 = jnp.where(sample & (r == 0), s1_ref[...], prev1)
    prev2 = jnp.where(sample & (r < 2), s2_ref[...], prev2)
    tail = cubuf[tm:tm + SUBLANES, :]
    cubuf[0:SUBLANES, :] = tail
    tail_ref[...] = tail

    @pl.when(sample)
    def _():
        cus_ref[...] = cu

    cw = cw_ref[...]
    y = cw[0:1, :] * prev2 + cw[1:2, :] * prev1 + cw[2:3, :] * cu
    h = x + _dot((b_gate * y).astype(BF16), wout_ref[...])
    _tok_store(h_ref, h)
    slab_ref[...], meta_ref[...] = _route_tail(h, nf_ref[...], wr_ref, brt_ref[...], carry_ref, n_groups, epg)
    cnt_ref[...] = carry_ref[...]


def _mixer_call(xp, xs, s1, s2, seq_len, s_len, na, win, cw, wout, nf, wr, brt, n_groups, epg):
    np_, d = xp.shape
    ns = xs.shape[0]
    tm = TOKEN_TILE
    p_tiles, s_tiles = np_ // tm, ns // tm
    n_tiles = p_tiles + s_tiles
    n = np_ + ns
    p_spec = pl.BlockSpec((tm, d), lambda i: (jnp.minimum(i, p_tiles - 1), 0))
    s_spec = pl.BlockSpec((tm, d), lambda i: (jnp.maximum(i - p_tiles, 0), 0))
    return pl.pallas_call(
        functools.partial(_mixer_kernel, tm=tm, d=d, p_tiles=p_tiles, tiles_per_seq=seq_len // tm, s_len=s_len,
                          n_groups=n_groups, epg=epg),
        grid=(n_tiles,),
        in_specs=[p_spec, s_spec, s_spec, s_spec, _const_spec(na.shape), _const_spec(win.shape),
                  _const_spec(cw.shape), _const_spec(wout.shape), _const_spec(nf.shape), _const_spec(wr.shape),
                  _const_spec(brt.shape)],
        out_specs=[_tok_spec(tm, d, lambda i: (i, 0)), pl.BlockSpec((tm, LANES), lambda i: (i, 0)),
                   pl.BlockSpec((SUBLANES, tm), lambda i: (i, 0)),
                   pl.BlockSpec((SUBLANES, d), lambda i: (i, 0)), s_spec,
                   pl.BlockSpec((1, LANES), lambda i: (0, 0))],
        out_shape=[jax.ShapeDtypeStruct((n * (d // LANES), LANES), F32), jax.ShapeDtypeStruct((n, LANES), F32),
                   jax.ShapeDtypeStruct((n_tiles * SUBLANES, tm), F32),
                   jax.ShapeDtypeStruct((n_tiles * SUBLANES, d), F32), jax.ShapeDtypeStruct((ns, d), F32),
                   jax.ShapeDtypeStruct((1, LANES), F32)],
        scratch_shapes=[pltpu.VMEM((tm + SUBLANES, d), F32), pltpu.VMEM((1, LANES), F32)],
        compiler_params=_cparams(1),
        name="mixer_a",
    )(xp, xs, s1, s2, na, win, cw, wout, nf, wr, brt)


def _invert_kernel(pos_ref, empty_hbm, inv_ref, *, n_pairs):
    pltpu.sync_copy(empty_hbm, inv_ref)

    def fill(p, c):
        inv_ref[pos_ref[p]] = p
        return c

    lax.fori_loop(0, n_pairs, fill, 0, unroll=DMA_UNROLL)


def _invert_call(pos, n_slots):
    n_pairs = pos.shape[0]
    grid_spec = pltpu.PrefetchScalarGridSpec(
        num_scalar_prefetch=1, grid=(1,), in_specs=[pl.BlockSpec(memory_space=pl.ANY)],
        out_specs=pl.BlockSpec(memory_space=pltpu.SMEM))
    return pl.pallas_call(
        functools.partial(_invert_kernel, n_pairs=n_pairs),
        grid_spec=grid_spec,
        out_shape=jax.ShapeDtypeStruct((n_slots,), I32),
        compiler_params=_cparams(1),
        name="invert_perm",
    )(pos, jnp.full((n_slots,), -1, I32))


def _experts_kernel(te_ref, nv_ref, src_ref, h_hbm, nf_ref, wg_ref, wu_ref, wd_ref, y_ref,
                    xbuf, wg_bf, wu_bf, wd_bf, gsem, *, tr, d):
    i = pl.program_id(0)
    nv = nv_ref[0]
    c = d // LANES

    def gather_rows(tile, s):
        base = tile * tr
        for r in range(tr):
            row0 = pl.multiple_of(src_ref[base + r], c)
            pltpu.make_async_copy(h_hbm.at[pl.ds(row0, c)], xbuf.at[s, pl.ds(r * c, c)],
                                  gsem.at[s]).start(priority=r % 2)

    def wait_gather(s):
        pltpu.make_async_copy(h_hbm.at[pl.ds(0, tr * c)], xbuf.at[s], gsem.at[s]).wait()

    def step(s):
        wait_gather(s)
        gather_rows(i + 1, 1 - s)
        prev = te_ref[jnp.maximum(i - 1, 0)]

        @pl.when((i == 0) | (te_ref[i] != prev))
        def _():
            wg_bf[...] = wg_ref[0].astype(BF16)
            wu_bf[...] = wu_ref[0].astype(BF16)
            wd_bf[...] = wd_ref[0].astype(BF16)

        x = _rms(_tok_load(xbuf.at[s], tr, d), nf_ref[...]).astype(BF16)
        g = _dot(x, wg_bf[...])
        u = _dot(x, wu_bf[...])
        hid = (g * (1.0 / (1.0 + jnp.exp(-g))) * u).astype(BF16)
        _tok_store(y_ref, _dot(hid, wd_bf[...]))

        @pl.when(i == nv - 1)
        def _():
            wait_gather(1 - s)

    @pl.when(i < nv)
    def _():
        @pl.when(i == 0)
        def _():
            gather_rows(0, 0)

        @pl.when(i % 2 == 0)
        def _():
            step(0)

        @pl.when(i % 2 == 1)
        def _():
            step(1)

    @pl.when(i >= nv)
    def _():
        y_ref[...] = jnp.zeros(y_ref.shape, F32)


def _experts_call(tile_expert, n_valid, src, h, nf, wg, wu, wd, layer, n_tiles):
    d, f = wg.shape[-2], wg.shape[-1]
    c = d // LANES
    tr = ROUTE_TILE
    w_spec = lambda shape: pl.BlockSpec(shape, lambda i, te, nv, s: (layer, te[i], 0, 0))
    grid_spec = pltpu.PrefetchScalarGridSpec(
        num_scalar_prefetch=3,
        grid=(n_tiles,),
        in_specs=[pl.BlockSpec(memory_space=pl.ANY),
                  pl.BlockSpec((1, d), lambda i, te, nv, s: (0, 0)),
                  w_spec((None, 1, d, f)), w_spec((None, 1, d, f)), w_spec((None, 1, f, d))],
        out_specs=pl.BlockSpec((tr * c, LANES), lambda i, te, nv, s: (i, 0)),
        scratch_shapes=[pltpu.VMEM((2, tr * c, LANES), F32),
                        pltpu.VMEM((d, f), BF16), pltpu.VMEM((d, f), BF16), pltpu.VMEM((f, d), BF16),
                        pltpu.SemaphoreType.DMA((2,))],
    )
    return pl.pallas_call(
        functools.partial(_experts_kernel, tr=tr, d=d),
        grid_spec=grid_spec,
        out_shape=jax.ShapeDtypeStruct((n_tiles * tr * c, LANES), F32),
        compiler_params=_cparams(1),
        name="moe_experts",
    )(tile_expert, n_valid, src, h, nf, wg, wu, wd)


def _fetch_pairs(pos_ref, y_hbm, ya_buf, yb_buf, sem, *, tm, n, d):
    i = pl.program_id(0)
    nt = pl.num_programs(0)
    c = d // LANES

    def start_tile(tile, s):
        base = tile * tm
        for r in range(tm):
            pa = pl.multiple_of(pos_ref[base + r], c)
            pb = pl.multiple_of(pos_ref[n + base + r], c)
            pltpu.make_async_copy(y_hbm.at[pl.ds(pa, c)], ya_buf.at[s, pl.ds(r * c, c)], sem.at[s]).start()
            pltpu.make_async_copy(y_hbm.at[pl.ds(pb, c)], yb_buf.at[s, pl.ds(r * c, c)],
                                  sem.at[s]).start(priority=1)

    def wait_tile(s):
        pltpu.make_async_copy(y_hbm.at[pl.ds(0, tm * c)], ya_buf.at[s], sem.at[s]).wait()
        pltpu.make_async_copy(y_hbm.at[pl.ds(0, tm * c)], yb_buf.at[s], sem.at[s]).wait()

    @pl.when(i == 0)
    def _():
        start_tile(0, 0)

    def fetch(s):
        wait_tile(s)
        start_tile(jnp.minimum(i + 1, nt - 1), 1 - s)

    def drain(s):
        @pl.when(i == nt - 1)
        def _():
            wait_tile(1 - s)

    return fetch, drain


def _by_slot(body):
    i = pl.program_id(0)

    @pl.when(i % 2 == 0)
    def _():
        body(0)

    @pl.when(i % 2 == 1)
    def _():
        body(1)


def _combine(h, slab, ya, yb):
    return h + (slab[:, 2:3] * ya + slab[:, 3:4] * yb)


def _kvq_kernel(pos_ref, h_ref, slab_ref, y_hbm, nkv_ref, wk_ref, wv_ref, wkt_ref, wvt_ref, wf_ref, bf_ref, nb_ref,
                wq_ref, h2_ref, ktp_ref, vtp_ref, ks_ref, vs_ref, kb_ref, vb_ref, lf_ref, q_ref,
                ya_buf, yb_buf, sem, *, tm, n, d, p_tiles, qs_prompt, qs_sample):
    prompt = pl.program_id(0) < p_tiles
    slot = pl.program_id(0) % 2
    fetch, drain = _fetch_pairs(pos_ref, y_hbm, ya_buf, yb_buf, sem, tm=tm, n=n, d=d)
    _by_slot(fetch)
    h2 = _combine(_tok_load(h_ref, tm, d), slab_ref[...],
                  _tok_load(ya_buf.at[slot], tm, d), _tok_load(yb_buf.at[slot], tm, d))
    _tok_store(h2_ref, h2)
    s = _rms(h2, nkv_ref[...]).astype(BF16)
    k = _dot(s, wk_ref[...])
    v = _dot(s, wv_ref[...])

    @pl.when(prompt)
    def _():
        ktp_ref[...] = _dot_nt(wkt_ref[...], s)
        vtp_ref[...] = _dot_nt(wvt_ref[...], s)

    @pl.when(jnp.logical_not(prompt))
    def _():
        ks_ref[...] = k
        vs_ref[...] = v

    kb_ref[...] = k.astype(BF16)
    vb_ref[...] = v.astype(BF16)
    z = _dot(s, wf_ref[...]) + bf_ref[...]
    lf_ref[...] = -(jnp.maximum(-z, 0.0) + jnp.log1p(jnp.exp(-jnp.abs(z))))
    qn = _rms(h2, nb_ref[...]).astype(BF16)
    q_scale = jnp.where(prompt, qs_prompt, qs_sample)
    q_ref[...] = (_dot(qn, wq_ref[...]) * q_scale).astype(BF16)
    _by_slot(drain)


def _kvq_call(pos, h, slab, ys, nkv, wk, wv, wkt, wvt, wf, bf, nb, wq, p_tiles, tiles_per_seq, qs_prompt,
              qs_sample):
    d, hd = wk.shape
    c = d // LANES
    n = h.shape[0] // c
    tm = KVQ_TILE
    nt = n // tm
    np_ = p_tiles * tm
    tok = lambda w: pl.BlockSpec((tm, w), lambda i, p: (i, 0))
    cst = lambda a: pl.BlockSpec(a.shape, lambda i, p: (0,) * a.ndim, pipeline_mode=pl.Buffered(1))
    res = _tok_spec(tm, d, lambda i, p: (i, 0))

    def p_map(i, p):
        ip = jnp.minimum(i, p_tiles - 1)
        return (ip // tiles_per_seq, 0, ip % tiles_per_seq)

    p_out = pl.BlockSpec((None, hd, tm), p_map)
    s_out = pl.BlockSpec((tm, hd), lambda i, p: (jnp.maximum(i - p_tiles, 0), 0))
    kt_shape = jax.ShapeDtypeStruct((p_tiles // tiles_per_seq, hd, tiles_per_seq * tm), F32)
    grid_spec = pltpu.PrefetchScalarGridSpec(
        num_scalar_prefetch=1,
        grid=(nt,),
        in_specs=[res, tok(LANES), pl.BlockSpec(memory_space=pl.ANY),
                  cst(nkv), cst(wk), cst(wv), cst(wkt), cst(wvt), cst(wf), cst(bf), cst(nb), cst(wq)],
        out_specs=[res, p_out, p_out, s_out, s_out, tok(hd), tok(hd), tok(LANES), tok(hd)],
        scratch_shapes=[pltpu.VMEM((2, tm * c, LANES), F32), pltpu.VMEM((2, tm * c, LANES), F32),
                        pltpu.SemaphoreType.DMA((2,))],
    )
    return pl.pallas_call(
        functools.partial(_kvq_kernel, tm=tm, n=n, d=d, p_tiles=p_tiles, qs_prompt=qs_prompt,
                          qs_sample=qs_sample),
        grid_spec=grid_spec,
        out_shape=[jax.ShapeDtypeStruct(h.shape, F32), kt_shape, kt_shape,
                   jax.ShapeDtypeStruct((n - np_, hd), F32),
                   jax.ShapeDtypeStruct((n - np_, hd), F32), jax.ShapeDtypeStruct((n, hd), BF16),
                   jax.ShapeDtypeStruct((n, hd), BF16), jax.ShapeDtypeStruct((n, LANES), F32),
                   jax.ShapeDtypeStruct((n, hd), BF16)],
        compiler_params=_cparams(1),
        name="combine_kvq",
    )(pos, h, slab, ys, nkv, wk, wv, wkt, wvt, wf, bf, nb, wq)


def _final_kernel(pos_ref, h_ref, slab_ref, y_hbm, nfin_ref, yp_ref, ys_ref, ya_buf, yb_buf, sem,
                  *, tm, n, d, p_tiles):
    slot = pl.program_id(0) % 2
    fetch, drain = _fetch_pairs(pos_ref, y_hbm, ya_buf, yb_buf, sem, tm=tm, n=n, d=d)
    _by_slot(fetch)
    h4 = _combine(_tok_load(h_ref, tm, d), slab_ref[...],
                  _tok_load(ya_buf.at[slot], tm, d), _tok_load(yb_buf.at[slot], tm, d))
    y = _rms(h4, nfin_ref[...])
    prompt = pl.program_id(0) < p_tiles

    @pl.when(prompt)
    def _():
        yp_ref[...] = y

    @pl.when(jnp.logical_not(prompt))
    def _():
        ys_ref[...] = y

    _by_slot(drain)


def _final_call(pos, h, slab, ys, nfin, p_tiles):
    d = nfin.shape[1]
    c = d // LANES
    n = h.shape[0] // c
    tm = KVQ_TILE
    nt = n // tm
    np_ = p_tiles * tm
    grid_spec = pltpu.PrefetchScalarGridSpec(
        num_scalar_prefetch=1,
        grid=(nt,),
        in_specs=[_tok_spec(tm, d, lambda i, p: (i, 0)), pl.BlockSpec((tm, LANES), lambda i, p: (i, 0)),
                  pl.BlockSpec(memory_space=pl.ANY),
                  pl.BlockSpec(nfin.shape, lambda i, p: (0, 0), pipeline_mode=pl.Buffered(1))],
        out_specs=[pl.BlockSpec((tm, d), lambda i, p: (jnp.minimum(i, p_tiles - 1), 0)),
                   pl.BlockSpec((tm, d), lambda i, p: (jnp.maximum(i - p_tiles, 0), 0))],
        scratch_shapes=[pltpu.VMEM((2, tm * c, LANES), F32), pltpu.VMEM((2, tm * c, LANES), F32),
                        pltpu.SemaphoreType.DMA((2,))],
    )
    return pl.pallas_call(
        functools.partial(_final_kernel, tm=tm, n=n, d=d, p_tiles=p_tiles),
        grid_spec=grid_spec,
        out_shape=[jax.ShapeDtypeStruct((np_, d), F32), jax.ShapeDtypeStruct((n - np_, d), F32)],
        compiler_params=_cparams(1),
        name="combine_final",
    )(pos, h, slab, ys, nfin)


def _cumsum_kernel(x_ref, c_ref, *, t):
    rb = x_ref.shape[0]
    row = lax.broadcasted_iota(I32, (LANES, LANES), 0)
    col = lax.broadcasted_iota(I32, (LANES, LANES), 1)
    tri = jnp.where(row <= col, 1.0, 0.0).astype(BF16)
    carry = jnp.zeros((rb, 1), F32)
    for j in range(t // LANES):
        blk = x_ref[:, j * LANES:(j + 1) * LANES]
        a1 = blk.astype(BF16)
        r1 = blk - a1.astype(F32)
        a2 = r1.astype(BF16)
        a3 = (r1 - a2.astype(F32)).astype(BF16)
        cs = _dot(a1, tri) + _dot(a2, tri) + _dot(a3, tri) + carry
        c_ref[:, j * LANES:(j + 1) * LANES] = cs
        carry = cs[:, LANES - 1:LANES]


def _cumsum_call(x):
    r, t = x.shape
    rb = min(r, 128)
    return pl.pallas_call(
        functools.partial(_cumsum_kernel, t=t),
        grid=(r // rb,),
        in_specs=[pl.BlockSpec((rb, t), lambda i: (i, 0))],
        out_specs=pl.BlockSpec((rb, t), lambda i: (i, 0)),
        out_shape=jax.ShapeDtypeStruct((r, t), F32),
        compiler_params=_cparams(1),
        name="logf_cumsum",
    )(x)


def _bias_columns(c2, base, lane, query_side):
    p1 = c2.astype(BF16).astype(F32)
    r1 = c2 - p1
    p2 = r1.astype(BF16).astype(F32)
    p3 = r1 - p2
    off = lane - base
    if query_side:
        ones = (off >= 3) & (off < 6)
        return jnp.where(off == 0, p1, jnp.where(off == 1, p2, jnp.where(off == 2, p3, jnp.where(ones, 1.0, 0.0))))
    ones = (off >= 0) & (off < 3)
    return jnp.where(off == 3, -p1, jnp.where(off == 4, -p2, jnp.where(off == 5, -p3, jnp.where(ones, 1.0, 0.0))))


def _attn_prompt_kernel(q_ref, k_ref, v_ref, c_ref, o_ref, kx0, kx1, vx0, vx1, qx_buf, s_buf, p_buf, m_buf, acc_buf,
                        *, blk, dh, nkv, strip):
    hp = pl.program_id(1)
    qi = pl.program_id(2)
    kx = (kx0, kx1)
    vx = (vx0, vx1)
    lane = lax.broadcasted_iota(I32, (blk, 2 * dh), 1)
    in_head = [(lane >= hd_ * dh) & (lane < (hd_ + 1) * dh) for hd_ in range(2)]
    ext_base = [dh, 0]

    def column(cs, h):
        return jnp.sum(jnp.where(lane == h, cs, 0.0), axis=-1, keepdims=True) * LOG2E

    @pl.when(qi == 0)
    def _():
        for jc in range(nkv):
            rows = slice(jc * blk, (jc + 1) * blk)
            kb = k_ref[rows, :]
            vb = v_ref[rows, :]
            cs = c_ref[rows, :]
            for hd_ in range(2):
                ext = _bias_columns(column(cs, 2 * hp + hd_), ext_base[hd_], lane, False)
                kx[hd_][rows, :] = jnp.where(in_head[hd_], kb, ext.astype(BF16))
                vx[hd_][rows, :] = jnp.where(in_head[hd_], vb, jnp.ones_like(vb))

    q = q_ref[...]
    cs_q = c_ref[pl.ds(pl.multiple_of(qi * blk, blk), blk), :]
    for hd_ in range(2):
        ext = _bias_columns(column(cs_q, 2 * hp + hd_), ext_base[hd_], lane, True)
        qx_buf[hd_] = jnp.where(in_head[hd_], q, ext.astype(BF16))
        m_buf[hd_] = jnp.full((blk, 1), -jnp.inf, F32)
        acc_buf[hd_] = jnp.zeros((blk, 2 * dh), F32)

    def scores(j):
        start = pl.multiple_of(j * blk, blk)
        for hd_ in range(2):
            s_buf[hd_] = _dot_nt(qx_buf[hd_], kx[hd_][pl.ds(start, blk), :])

    def softmax(masked):
        for hd_ in range(2):
            for r0 in range(0, blk, strip):
                rows = slice(r0, r0 + strip)
                s = s_buf[hd_, rows, :]
                if masked:
                    row = lax.broadcasted_iota(I32, (strip, blk), 0) + r0
                    col = lax.broadcasted_iota(I32, (strip, blk), 1)
                    s = jnp.where(col <= row, s, -jnp.inf)
                m_old = m_buf[hd_, rows, :]
                m_new = jnp.maximum(m_old, jnp.max(s, axis=-1, keepdims=True))
                p_buf[hd_, rows, :] = jnp.exp2(s - m_new).astype(BF16)
                m_buf[hd_, rows, :] = m_new
                acc_buf[hd_, rows, :] = jnp.exp2(m_old - m_new) * acc_buf[hd_, rows, :]

    def values(j):
        start = pl.multiple_of(j * blk, blk)
        for hd_ in range(2):
            acc_buf[hd_] += _dot(p_buf[hd_], vx[hd_][pl.ds(start, blk), :])

    def body(j, carry):
        softmax(False)
        scores(j + 1)
        values(j)
        return carry

    scores(0)
    lax.fori_loop(0, qi, body, 0)
    softmax(True)
    values(qi)
    acc0 = acc_buf[0]
    acc1 = acc_buf[1]
    o0 = acc0 / pltpu.roll(acc0, dh, 1)
    o1 = acc1 / pltpu.roll(acc1, dh, 1)
    o_ref[...] = jnp.where(in_head[0], o0, o1).astype(BF16)


def _attn_prompt_call(q, kb, vb, c_slab, batch, seq, n_heads, dh):
    blk = min(ATTN_BLOCK, seq)
    nq = seq // blk
    hd = n_heads * dh
    pair = pl.BlockSpec((seq, 2 * dh), lambda b, hp, qi: (b, hp))
    return pl.pallas_call(
        functools.partial(_attn_prompt_kernel, blk=blk, dh=dh, nkv=nq, strip=min(ATTN_STRIP, blk)),
        grid=(batch, n_heads // 2, nq),
        in_specs=[pl.BlockSpec((blk, 2 * dh), lambda b, hp, qi: (b * nq + qi, hp)), pair, pair,
                  pl.BlockSpec((seq, LANES), lambda b, hp, qi: (b, 0))],
        out_specs=pl.BlockSpec((blk, 2 * dh), lambda b, hp, qi: (b * nq + qi, hp)),
        out_shape=jax.ShapeDtypeStruct((batch * seq, hd), BF16),
        scratch_shapes=[pltpu.VMEM((seq, 2 * dh), BF16)] * 4 + [
            pltpu.VMEM((2, blk, 2 * dh), BF16), pltpu.VMEM((2, blk, blk), F32), pltpu.VMEM((2, blk, blk), BF16),
            pltpu.VMEM((2, blk, 1), F32), pltpu.VMEM((2, blk, 2 * dh), F32)],
        compiler_params=_cparams(3),
        name="attn_prompt",
    )(q, kb, vb, c_slab)


def _attn_sample_kernel(q_ref, kc_ref, vc_ref, kn_ref, vn_ref, cq_ref, ck_ref, o_ref,
                        qbd, m_ref, l_ref, acc_ref, *, ts, n_heads, dh, nchunk):
    j = pl.program_id(1)
    r = n_heads * ts
    hd = n_heads * dh

    @pl.when(j == 0)
    def _():
        q = q_ref[...]
        qt = jnp.concatenate([q] * n_heads, axis=0)
        rr = lax.broadcasted_iota(I32, (r, hd), 0) // ts
        ll = lax.broadcasted_iota(I32, (r, hd), 1) // dh
        qbd[...] = jnp.where(rr == ll, qt, jnp.zeros_like(qt))
        m_ref[...] = jnp.full((r, 1), -jnp.inf, F32)
        l_ref[...] = jnp.zeros((r, 1), F32)
        acc_ref[...] = jnp.zeros((r, hd), F32)

    def expand(ck):
        w = ck.shape[1]
        return jnp.concatenate([jnp.broadcast_to(ck[h:h + 1, :], (ts, w)) for h in range(n_heads)], axis=0)

    def update(s, pv):
        m = m_ref[...]
        m_new = jnp.maximum(m, jnp.max(s, axis=-1, keepdims=True))
        p = jnp.exp(s - m_new)
        alpha = jnp.exp(m - m_new)
        l_ref[...] = alpha * l_ref[...] + jnp.sum(p, axis=-1, keepdims=True)
        acc_ref[...] = alpha * acc_ref[...] + pv(p.astype(BF16))
        m_ref[...] = m_new

    cq = cq_ref[...][:, 0:1]

    @pl.when(j < nchunk)
    def _():
        w = kc_ref.shape[-1]
        kt = kc_ref[...].reshape(hd, w).astype(BF16)
        vt = vc_ref[...].reshape(hd, w).astype(BF16)
        s = _dot(qbd[...], kt) + (cq - expand(ck_ref[...]))
        update(s, lambda p: _dot_nt(p, vt))

    @pl.when(j == nchunk)
    def _():
        s = _dot_nt(qbd[...], kn_ref[...]) + (cq - expand(ck_ref[:, 0:ts]))
        qpos = lax.broadcasted_iota(I32, (r, ts), 0) % ts
        kpos = lax.broadcasted_iota(I32, (r, ts), 1)
        s = jnp.where(kpos <= qpos, s, -jnp.inf)
        update(s, lambda p: _dot(p, vn_ref[...]))
        o = acc_ref[...] / l_ref[...]
        ll = lax.broadcasted_iota(I32, (ts, hd), 1) // dh
        out = jnp.zeros((ts, hd), F32)
        for h in range(n_heads):
            out = jnp.where(ll == h, o[h * ts:(h + 1) * ts, :], out)
        o_ref[...] = out.astype(BF16)


def _attn_sample_call(q, cache_kt, cache_vt, k_new, v_new, cq_rep, ck_chunks, row0, ts):
    n_streams, n_heads, dh, past = cache_kt.shape
    hd = n_heads * dh
    ck_len = min(CACHE_CHUNK, past)
    nchunk = past // ck_len
    r = n_heads * ts
    b0 = row0 // ts
    cache_spec = pl.BlockSpec((None, n_heads, dh, ck_len), lambda b, j: (b, 0, 0, jnp.minimum(j, nchunk - 1)))
    new_spec = pl.BlockSpec((ts, hd), lambda b, j: (b0 + b, 0))
    return pl.pallas_call(
        functools.partial(_attn_sample_kernel, ts=ts, n_heads=n_heads, dh=dh, nchunk=nchunk),
        grid=(n_streams, nchunk + 1),
        in_specs=[new_spec, cache_spec, cache_spec, new_spec, new_spec,
                  pl.BlockSpec((r, LANES), lambda b, j: (b, 0)),
                  pl.BlockSpec((n_heads, ck_len), lambda b, j: (b * (nchunk + 1) + j, 0))],
        out_specs=pl.BlockSpec((ts, hd), lambda b, j: (b, 0)),
        out_shape=jax.ShapeDtypeStruct((n_streams * ts, hd), BF16),
        scratch_shapes=[pltpu.VMEM((r, hd), BF16), pltpu.VMEM((r, 1), F32), pltpu.VMEM((r, 1), F32),
                        pltpu.VMEM((r, hd), F32)],
        compiler_params=_cparams(2),
        name="attn_sample",
    )(q, cache_kt, cache_vt, k_new, v_new, cq_rep, ck_chunks)


def _wo_kernel(h_ref, op_ref, os_ref, wo_ref, nf_ref, wr_ref, brt_ref, h3_ref, slab_ref, meta_ref, cnt_ref,
               carry_ref, *, tm, d, p_tiles, n_groups, epg):
    i = pl.program_id(0)

    @pl.when(i == 0)
    def _():
        carry_ref[...] = jnp.zeros(carry_ref.shape, F32)

    o = jnp.where(i >= p_tiles, os_ref[...], op_ref[...])
    h3 = _tok_load(h_ref, tm, d) + _dot(o, wo_ref[...])
    _tok_store(h3_ref, h3)
    slab_ref[...], meta_ref[...] = _route_tail(h3, nf_ref[...], wr_ref, brt_ref[...], carry_ref, n_groups, epg)
    cnt_ref[...] = carry_ref[...]


def _wo_call(h, o_p, o_s, wo, nf, wr, brt, n_groups, epg):
    hd, d = wo.shape
    n = h.shape[0] // (d // LANES)
    tm = TOKEN_TILE
    p_tiles = o_p.shape[0] // tm
    res = _tok_spec(tm, d, lambda i: (i, 0))
    return pl.pallas_call(
        functools.partial(_wo_kernel, tm=tm, d=d, p_tiles=p_tiles, n_groups=n_groups, epg=epg),
        grid=(n // tm,),
        in_specs=[res,
                  pl.BlockSpec((tm, hd), lambda i: (jnp.minimum(i, p_tiles - 1), 0)),
                  pl.BlockSpec((tm, hd), lambda i: (jnp.maximum(i - p_tiles, 0), 0)),
                  _const_spec(wo.shape), _const_spec(nf.shape), _const_spec(wr.shape), _const_spec(brt.shape)],
        out_specs=[res, pl.BlockSpec((tm, LANES), lambda i: (i, 0)),
                   pl.BlockSpec((SUBLANES, tm), lambda i: (i, 0)), pl.BlockSpec((1, LANES), lambda i: (0, 0))],
        out_shape=[jax.ShapeDtypeStruct(h.shape, F32), jax.ShapeDtypeStruct((n, LANES), F32),
                   jax.ShapeDtypeStruct((n // tm * SUBLANES, tm), F32), jax.ShapeDtypeStruct((1, LANES), F32)],
        scratch_shapes=[pltpu.VMEM((1, LANES), F32)],
        compiler_params=_cparams(1),
        name="wo_router",
    )(h, o_p, o_s, wo, nf, wr, brt)


def _router_weights(w_group, b_group, w_router, b_router):
    d = w_group.shape[0]
    n = w_group.shape[1] + w_router.shape[1]
    w = jnp.zeros((d, LANES), F32).at[:, :n].set(jnp.concatenate([w_group, w_router], axis=1))
    w1 = w.astype(BF16)
    w2 = (w - w1.astype(F32)).astype(BF16)
    b = jnp.zeros((1, LANES), F32).at[0, :n].set(jnp.concatenate([b_group, b_router]))
    return jnp.stack([w1, w2]), b


def _moe(h, meta, counts, nf, wg, wu, wd, layer, n_groups, n_experts):
    tile = ROUTE_TILE
    c = wg.shape[-2] // LANES
    n = h.shape[0] // c
    n_pairs = 2 * n
    cnt = counts[0, n_groups:n_groups + n_experts].astype(I32)
    padded = ((cnt + tile - 1) // tile) * tile
    ends = jnp.cumsum(padded)
    offs = ends - padded
    n_tiles = (n_pairs + n_experts * (tile - 1) + tile - 1) // tile
    n_valid = (ends[-1] // tile).astype(I32)
    starts = jnp.arange(n_tiles, dtype=I32) * tile
    te = jnp.zeros((n_tiles,), I32)
    for e in range(n_experts - 1):
        te = te + (starts >= ends[e]).astype(I32)
    te = jnp.where(jnp.arange(n_tiles) < n_valid, te, te[jnp.maximum(n_valid - 1, 0)])
    m = meta.reshape(-1, SUBLANES, meta.shape[1]).astype(I32)
    ids = m[:, 0:2, :].transpose(1, 0, 2).reshape(2, n)
    ranks = m[:, 4:6, :].transpose(1, 0, 2).reshape(2, n)
    base = jnp.zeros((2, n), I32)
    for e in range(n_experts):
        base = jnp.where(ids == e, offs[e], base)
    pos = (base + ranks).reshape(n_pairs)
    n_slots = (n_tiles + 1) * tile
    inv = _invert_call(pos, n_slots)
    src = jnp.where(inv >= 0, jnp.where(inv >= n, inv - n, inv), 0)
    ys = _experts_call(te, n_valid.reshape(1), src * c, h, nf, wg, wu, wd, layer, n_tiles)
    return pos * c, ys


def kernel(x_prompt, x_sample, state_conv, cache_k, cache_v, cache_logf, norm_a, w_in_a, conv_w_a, w_out_a,
           norm_kv, w_k, w_v, w_f, b_f, norm_b, w_q_b, w_o_b, norm_ffn, w_group, b_group, w_router, b_router,
           w_gate, w_up, w_down, norm_final):
    bp, tp, d = x_prompt.shape
    bs, ts, _ = x_sample.shape
    past, n_heads, dh = cache_k.shape[1], cache_k.shape[2], cache_k.shape[3]
    hd = n_heads * dh
    n_groups = w_group.shape[-1]
    n_experts = w_gate.shape[1]
    epg = n_experts // n_groups
    np_, ns = bp * tp, bs * ts
    tm = TOKEN_TILE
    assert state_conv.shape[0] == 1 and w_q_b.shape[0] == 1 and state_conv.shape[2] == 2
    assert dh * 2 == LANES and n_heads % 2 == 0
    assert tp % tm == 0 and ns % tm == 0 and tm % ts == 0 and ts >= 2
    p_tiles = np_ // tm

    row = lambda a: a.reshape(1, -1).astype(F32)
    win = w_in_a[0].astype(BF16)
    wout = w_out_a[0].astype(BF16)
    wk, wv, wq, wo = w_k.astype(BF16), w_v.astype(BF16), w_q_b[0].astype(BF16), w_o_b[0].astype(BF16)
    wf = jnp.zeros((d, LANES), F32).at[:, :n_heads].set(w_f).astype(BF16)
    bf = jnp.zeros((1, LANES), F32).at[0, :n_heads].set(b_f)
    wr0, br0 = _router_weights(w_group[0], b_group[0], w_router[0], b_router[0])
    wr1, br1 = _router_weights(w_group[1], b_group[1], w_router[1], b_router[1])

    st = state_conv[0]
    s1 = jnp.zeros((bs, ts, d), F32).at[:, 0].set(st[:, 1]).reshape(ns, d)
    s2 = jnp.zeros((bs, ts, d), F32).at[:, 0].set(st[:, 0]).at[:, 1].set(st[:, 1]).reshape(ns, d)

    h1, slab0, meta0, tails, cus, cnt0 = _mixer_call(
        x_prompt.reshape(np_, d), x_sample.reshape(ns, d), s1, s2, tp, ts, row(norm_a[0]), win, conv_w_a[0], wout,
        row(norm_ffn[0]), wr0, br0, n_groups, epg)
    conv_prompt = tails[:p_tiles * SUBLANES].reshape(bp, tp // tm, SUBLANES, d)[:, -1, SUBLANES - 2:][None]
    conv_sample = cus.reshape(bs, ts, d)[:, ts - 2:][None]
    pos0, ys0 = _moe(h1, meta0, cnt0, row(norm_ffn[0]), w_gate, w_up, w_down, 0, n_groups, n_experts)

    h2, kt_p, vt_p, k_s, v_s, kb_all, vb_all, lf_all, q_all = _kvq_call(
        pos0, h1, slab0, ys0, row(norm_kv), wk, wv, w_k.T.astype(BF16), w_v.T.astype(BF16), wf, bf, row(norm_b[0]), wq,
        np_ // KVQ_TILE, tp // KVQ_TILE, float(dh) ** -0.5 * LOG2E, float(dh) ** -0.5)
    k_prompt = kt_p.reshape(bp, n_heads, dh, tp).transpose(0, 3, 1, 2)
    v_prompt = vt_p.reshape(bp, n_heads, dh, tp).transpose(0, 3, 1, 2)
    logf_prompt = lf_all[:np_, :n_heads].reshape(bp, tp, n_heads)
    logf_sample = lf_all[np_:, :n_heads].reshape(bs, ts, n_heads)

    c_p = _cumsum_call(logf_prompt.transpose(0, 2, 1).reshape(bp * n_heads, tp))
    c_slab = jnp.zeros((np_, LANES), F32).at[:, :n_heads].set(
        c_p.reshape(bp, n_heads, tp).transpose(0, 2, 1).reshape(np_, n_heads))
    o_p = _attn_prompt_call(q_all, kb_all, vb_all, c_slab, bp, tp, n_heads, dh)

    tall = past + ts
    tpad = ((tall + LANES - 1) // LANES) * LANES
    lfs = jnp.concatenate([cache_logf.astype(F32), logf_sample], axis=1).transpose(0, 2, 1)
    lfs = jnp.pad(lfs, ((0, 0), (0, 0), (0, tpad - tall))).reshape(bs * n_heads, tpad)
    c_s = _cumsum_call(lfs).reshape(bs, n_heads, tpad)
    ck_len = min(CACHE_CHUNK, past)
    nchunk = past // ck_len
    c_past = c_s[:, :, :past].reshape(bs, n_heads, nchunk, ck_len).transpose(0, 2, 1, 3)
    c_new = c_s[:, :, past:past + ts]
    c_new_pad = jnp.pad(c_new, ((0, 0), (0, 0), (0, ck_len - ts)))[:, None]
    ck_s = jnp.concatenate([c_past, c_new_pad], axis=1).reshape(bs * (nchunk + 1) * n_heads, ck_len)
    cq_s = jnp.broadcast_to(c_new.reshape(bs * n_heads * ts, 1), (bs * n_heads * ts, LANES))
    o_s = _attn_sample_call(q_all, cache_k.transpose(0, 2, 3, 1), cache_v.transpose(0, 2, 3, 1), kb_all, vb_all,
                            cq_s, ck_s, np_, ts)

    h3, slab1, meta1, cnt1 = _wo_call(h2, o_p, o_s, wo, row(norm_ffn[1]), wr1, br1, n_groups, epg)
    pos1, ys1 = _moe(h3, meta1, cnt1, row(norm_ffn[1]), w_gate, w_up, w_down, 1, n_groups, n_experts)
    y_p, y_s = _final_call(pos1, h3, slab1, ys1, row(norm_final), np_ // KVQ_TILE)

    return (y_p.reshape(bp, tp, d), y_s.reshape(bs, ts, d), conv_prompt, conv_sample,
            k_prompt, v_prompt, logf_prompt,
            k_s.reshape(bs, ts, n_heads, dh), v_s.reshape(bs, ts, n_heads, dh), logf_sample)
```

```python
import functools

import jax
import jax.numpy as jnp
from jax import lax
from jax.experimental import pallas as pl
from jax.experimental.pallas import tpu as pltpu

F32 = jnp.float32
BF16 = jnp.bfloat16
I32 = jnp.int32

RMS_EPS = 1e-6
LOG2E = 1.4426950408889634
LANES = 128
SUBLANES = 8
ROUTE_TILE = 256
TOKEN_TILE = 512
KVQ_TILE = 256
ATTN_BLOCK = 512
ATTN_STRIP = 64
CACHE_CHUNK = 1024
DMA_UNROLL = 8
VMEM_LIMIT = 56 * 1024 * 1024


def _cparams(n_axes):
    return pltpu.CompilerParams(dimension_semantics=("arbitrary",) * n_axes,
                                vmem_limit_bytes=VMEM_LIMIT)


def _rms(x, g):
    return x * lax.rsqrt(jnp.mean(x * x, axis=-1, keepdims=True) + RMS_EPS) * g


def _dot(a, b):
    return jnp.dot(a, b, preferred_element_type=F32)


def _dot_nt(a, b):
    return lax.dot_general(a, b, (((1,), (1,)), ((), ())), preferred_element_type=F32)


def _tok_load(ref, rows, d):
    c = d // LANES
    return jnp.concatenate([ref[pl.ds(j, rows, stride=c), :] for j in range(c)], axis=1)


def _tok_store(ref, val):
    rows, d = val.shape
    c = d // LANES
    for j in range(c):
        ref[pl.ds(j, rows, stride=c), :] = val[:, j * LANES:(j + 1) * LANES]


def _tok_spec(rows, d, index_map):
    return pl.BlockSpec((rows * (d // LANES), LANES), index_map)


def _const_spec(shape):
    nd = len(shape)
    return pl.BlockSpec(shape, lambda *_: (0,) * nd, pipeline_mode=pl.Buffered(1))


def _route_tail(h, g_ffn, wr_ref, b_rt, carry_ref, n_groups, epg):
    tm = h.shape[0]
    hn = _rms(h, g_ffn)
    a1 = hn.astype(BF16)
    a2 = (hn - a1.astype(F32)).astype(BF16)
    w1 = wr_ref[0]
    w2 = wr_ref[1]
    logits = _dot(a1, w1) + _dot(a1, w2) + _dot(a2, w1) + b_rt
    lane = lax.broadcasted_iota(I32, logits.shape, 1)
    lanef = lane.astype(F32)
    neg = -jnp.inf
    big = 1e9
    gl = jnp.where(lane < n_groups, logits, neg)
    gmax = jnp.max(gl, axis=-1, keepdims=True)
    g_idx = jnp.min(jnp.where(gl == gmax, lanef, big), axis=-1, keepdims=True)
    g_w = 1.0 / jnp.sum(jnp.exp(gl - gmax), axis=-1, keepdims=True)
    lo = n_groups + g_idx * epg
    el = jnp.where((lanef >= lo) & (lanef < lo + epg), logits, neg)
    m1 = jnp.max(el, axis=-1, keepdims=True)
    i1 = jnp.min(jnp.where(el == m1, lanef, big), axis=-1, keepdims=True)
    el2 = jnp.where(lanef == i1, neg, el)
    m2 = jnp.max(el2, axis=-1, keepdims=True)
    i2 = jnp.min(jnp.where(el2 == m2, lanef, big), axis=-1, keepdims=True)
    t = jnp.exp(m2 - m1)
    cw1 = g_w * (1.0 / (1.0 + t))
    cw2 = g_w * (t / (1.0 + t))
    sel1 = lanef == i1
    sel2 = lanef == i2
    oh = jnp.where(sel1 | sel2, 1.0, 0.0)
    row = lax.broadcasted_iota(I32, (tm, tm), 0)
    col = lax.broadcasted_iota(I32, (tm, tm), 1)
    tri = jnp.where(col < row, 1.0, 0.0).astype(BF16)
    carry = carry_ref[...]
    prefix = _dot(tri, oh.astype(BF16)) + carry
    rank1 = jnp.sum(jnp.where(sel1, prefix, 0.0), axis=-1, keepdims=True)
    rank2 = jnp.sum(jnp.where(sel2, prefix, 0.0), axis=-1, keepdims=True)
    carry_ref[...] = carry + jnp.sum(oh, axis=0, keepdims=True)
    slab = jnp.where(lane == 0, i1 - n_groups,
           jnp.where(lane == 1, i2 - n_groups,
           jnp.where(lane == 2, cw1,
           jnp.where(lane == 3, cw2,
           jnp.where(lane == 4, rank1,
           jnp.where(lane == 5, rank2, 0.0))))))
    return slab, jnp.transpose(slab)[0:SUBLANES, :]


def _mixer_kernel(xp_ref, xs_ref, s1_ref, s2_ref, na_ref, win_ref, cw_ref, wout_ref, nf_ref, wr_ref, brt_ref,
                  h_ref, slab_ref, meta_ref, tail_ref, cus_ref, cnt_ref, cubuf, carry_ref,
                  *, tm, d, p_tiles, tiles_per_seq, s_len, n_groups, epg):
    i = pl.program_id(0)
    sample = i >= p_tiles

    @pl.when(i == 0)
    def _():
        carry_ref[...] = jnp.zeros(carry_ref.shape, F32)

    @pl.when(sample | (i % tiles_per_seq == 0))
    def _():
        cubuf[0:SUBLANES, :] = jnp.zeros((SUBLANES, d), F32)

    x = jnp.where(sample, xs_ref[...], xp_ref[...])
    xn = _rms(x, na_ref[...]).astype(BF16)
    bcu = _dot(xn, win_ref[...])
    b_gate = bcu[:, 0:d]
    cu = bcu[:, d:2 * d] * bcu[:, 2 * d:3 * d]
    cubuf[SUBLANES:SUBLANES + tm, :] = cu
    prev1 = cubuf[SUBLANES - 1:SUBLANES - 1 + tm, :]
    prev2 = cubuf[SUBLANES - 2:SUBLANES - 2 + tm, :]
    r = lax.broadcasted_iota(I32, (tm, d), 0) % s_len
    prev1 = jnp.where(sample & (r == 0), s1_ref[...], prev1)
    prev2 = jnp.where(sample & (r < 2), s2_ref[...], prev2)
    tail = cubuf[tm:tm + SUBLANES, :]
    cubuf[0:SUBLANES, :] = tail
    tail_ref[...] = tail

    @pl.when(sample)
    def _():
        cus_ref[...] = cu

    cw = cw_ref[...]
    y = cw[0:1, :] * prev2 + cw[1:2, :] * prev1 + cw[2:3, :] * cu
    h = x + _dot((b_gate * y).astype(BF16), wout_ref[...])
    _tok_store(h_ref, h)
    slab_ref[...], meta_ref[...] = _route_tail(h, nf_ref[...], wr_ref, brt_ref[...], carry_ref, n_groups, epg)
    cnt_ref[...] = carry_ref[...]


def _mixer_call(xp, xs, s1, s2, seq_len, s_len, na, win, cw, wout, nf, wr, brt, n_groups, epg):
    np_, d = xp.shape
    ns = xs.shape[0]
    tm = TOKEN_TILE
    p_tiles, s_tiles = np_ // tm, ns // tm
    n_tiles = p_tiles + s_tiles
    n = np_ + ns
    p_spec = pl.BlockSpec((tm, d), lambda i: (jnp.minimum(i, p_tiles - 1), 0))
    s_spec = pl.BlockSpec((tm, d), lambda i: (jnp.maximum(i - p_tiles, 0), 0))
    return pl.pallas_call(
        functools.partial(_mixer_kernel, tm=tm, d=d, p_tiles=p_tiles, tiles_per_seq=seq_len // tm, s_len=s_len,
                          n_groups=n_groups, epg=epg),
        grid=(n_tiles,),
        in_specs=[p_spec, s_spec, s_spec, s_spec, _const_spec(na.shape), _const_spec(win.shape),
                  _const_spec(cw.shape), _const_spec(wout.shape), _const_spec(nf.shape), _const_spec(wr.shape),
                  _const_spec(brt.shape)],
        out_specs=[_tok_spec(tm, d, lambda i: (i, 0)), pl.BlockSpec((tm, LANES), lambda i: (i, 0)),
                   pl.BlockSpec((SUBLANES, tm), lambda i: (i, 0)),
                   pl.BlockSpec((SUBLANES, d), lambda i: (i, 0)), s_spec,
                   pl.BlockSpec((1, LANES), lambda i: (0, 0))],
        out_shape=[jax.ShapeDtypeStruct((n * (d // LANES), LANES), F32), jax.ShapeDtypeStruct((n, LANES), F32),
                   jax.ShapeDtypeStruct((n_tiles * SUBLANES, tm), F32),
                   jax.ShapeDtypeStruct((n_tiles * SUBLANES, d), F32), jax.ShapeDtypeStruct((ns, d), F32),
                   jax.ShapeDtypeStruct((1, LANES), F32)],
        scratch_shapes=[pltpu.VMEM((tm + SUBLANES, d), F32), pltpu.VMEM((1, LANES), F32)],
        compiler_params=_cparams(1),
        name="mixer_a",
    )(xp, xs, s1, s2, na, win, cw, wout, nf, wr, brt)


def _dispatch_kernel(pos_ref, h_ref, xs_in, xs_out, sem, *, tm, n, d):
    del xs_in
    c = d // LANES
    base = pl.program_id(0) * tm
    for r in range(tm):
        src = h_ref.at[pl.ds(r * c, c)]
        pa = pl.multiple_of(pos_ref[base + r], c)
        pb = pl.multiple_of(pos_ref[n + base + r], c)
        pltpu.make_async_copy(src, xs_out.at[pl.ds(pa, c)], sem).start()
        pltpu.make_async_copy(src, xs_out.at[pl.ds(pb, c)], sem).start(priority=1)
    for _ in range(2):
        pltpu.make_async_copy(h_ref, xs_out.at[pl.ds(0, tm * c)], sem).wait()


def _dispatch_call(pos, h, n_slots, d):
    c = d // LANES
    n = h.shape[0] // c
    tm = KVQ_TILE
    xs = jnp.zeros((n_slots * c, LANES), F32)
    grid_spec = pltpu.PrefetchScalarGridSpec(
        num_scalar_prefetch=1,
        grid=(n // tm,),
        in_specs=[_tok_spec(tm, d, lambda i, p: (i, 0)), pl.BlockSpec(memory_space=pl.ANY)],
        out_specs=pl.BlockSpec(memory_space=pl.ANY),
        scratch_shapes=[pltpu.SemaphoreType.DMA(())],
    )
    return pl.pallas_call(
        functools.partial(_dispatch_kernel, tm=tm, n=n, d=d),
        grid_spec=grid_spec,
        out_shape=jax.ShapeDtypeStruct(xs.shape, F32),
        input_output_aliases={2: 0},
        compiler_params=_cparams(1),
        name="moe_dispatch",
    )(pos, h, xs)


def _experts_kernel(te_ref, nv_ref, x_ref, nf_ref, wg_ref, wu_ref, wd_ref, y_ref, wg_bf, wu_bf, wd_bf, *, tr, d):
    i = pl.program_id(0)
    nv = nv_ref[0]

    @pl.when(i < nv)
    def _():
        prev = te_ref[jnp.maximum(i - 1, 0)]

        @pl.when((i == 0) | (te_ref[i] != prev))
        def _():
            wg_bf[...] = wg_ref[0].astype(BF16)
            wu_bf[...] = wu_ref[0].astype(BF16)
            wd_bf[...] = wd_ref[0].astype(BF16)

        x = _rms(_tok_load(x_ref, tr, d), nf_ref[...]).astype(BF16)
        g = _dot(x, wg_bf[...])
        u = _dot(x, wu_bf[...])
        hid = (g * (1.0 / (1.0 + jnp.exp(-g))) * u).astype(BF16)
        _tok_store(y_ref, _dot(hid, wd_bf[...]))

    @pl.when(i >= nv)
    def _():
        y_ref[...] = jnp.zeros(y_ref.shape, F32)


def _experts_call(tile_expert, n_valid, xs, nf, wg, wu, wd, layer, n_tiles):
    d, f = wg.shape[-2], wg.shape[-1]
    c = d // LANES
    tr = ROUTE_TILE
    w_spec = lambda shape: pl.BlockSpec(shape, lambda i, te, nv: (layer, te[i], 0, 0))
    grid_spec = pltpu.PrefetchScalarGridSpec(
        num_scalar_prefetch=2,
        grid=(n_tiles,),
        in_specs=[pl.BlockSpec((tr * c, LANES), lambda i, te, nv: (jnp.minimum(i, nv[0] - 1), 0)),
                  pl.BlockSpec((1, d), lambda i, te, nv: (0, 0)),
                  w_spec((None, 1, d, f)), w_spec((None, 1, d, f)), w_spec((None, 1, f, d))],
        out_specs=pl.BlockSpec((tr * c, LANES), lambda i, te, nv: (i, 0)),
        scratch_shapes=[pltpu.VMEM((d, f), BF16), pltpu.VMEM((d, f), BF16), pltpu.VMEM((f, d), BF16)],
    )
    return pl.pallas_call(
        functools.partial(_experts_kernel, tr=tr, d=d),
        grid_spec=grid_spec,
        out_shape=jax.ShapeDtypeStruct((n_tiles * tr * c, LANES), F32),
        compiler_params=_cparams(1),
        name="moe_experts",
    )(tile_expert, n_valid, xs, nf, wg, wu, wd)


def _fetch_pairs(pos_ref, y_hbm, ya_buf, yb_buf, sem, *, tm, n, d):
    i = pl.program_id(0)
    nt = pl.num_programs(0)
    c = d // LANES

    def start_tile(tile, s):
        base = tile * tm
        for r in range(tm):
            pa = pl.multiple_of(pos_ref[base + r], c)
            pb = pl.multiple_of(pos_ref[n + base + r], c)
            pltpu.make_async_copy(y_hbm.at[pl.ds(pa, c)], ya_buf.at[s, pl.ds(r * c, c)], sem.at[s]).start()
            pltpu.make_async_copy(y_hbm.at[pl.ds(pb, c)], yb_buf.at[s, pl.ds(r * c, c)],
                                  sem.at[s]).start(priority=1)

    def wait_tile(s):
        pltpu.make_async_copy(y_hbm.at[pl.ds(0, tm * c)], ya_buf.at[s], sem.at[s]).wait()
        pltpu.make_async_copy(y_hbm.at[pl.ds(0, tm * c)], yb_buf.at[s], sem.at[s]).wait()

    @pl.when(i == 0)
    def _():
        start_tile(0, 0)

    def fetch(s):
        wait_tile(s)
        start_tile(jnp.minimum(i + 1, nt - 1), 1 - s)

    def drain(s):
        @pl.when(i == nt - 1)
        def _():
            wait_tile(1 - s)

    return fetch, drain


def _by_slot(body):
    i = pl.program_id(0)

    @pl.when(i % 2 == 0)
    def _():
        body(0)

    @pl.when(i % 2 == 1)
    def _():
        body(1)


def _combine(h, slab, ya, yb):
    return h + (slab[:, 2:3] * ya + slab[:, 3:4] * yb)


def _kvq_kernel(pos_ref, h_ref, slab_ref, y_hbm, nkv_ref, wk_ref, wv_ref, wkt_ref, wvt_ref, wf_ref, bf_ref, nb_ref,
                wq_ref, h2_ref, ktp_ref, vtp_ref, ks_ref, vs_ref, kb_ref, vb_ref, lf_ref, q_ref,
                ya_buf, yb_buf, sem, *, tm, n, d, p_tiles, qs_prompt, qs_sample):
    prompt = pl.program_id(0) < p_tiles
    slot = pl.program_id(0) % 2
    fetch, drain = _fetch_pairs(pos_ref, y_hbm, ya_buf, yb_buf, sem, tm=tm, n=n, d=d)
    _by_slot(fetch)
    h2 = _combine(_tok_load(h_ref, tm, d), slab_ref[...],
                  _tok_load(ya_buf.at[slot], tm, d), _tok_load(yb_buf.at[slot], tm, d))
    _tok_store(h2_ref, h2)
    s = _rms(h2, nkv_ref[...]).astype(BF16)
    k = _dot(s, wk_ref[...])
    v = _dot(s, wv_ref[...])

    @pl.when(prompt)
    def _():
        ktp_ref[...] = _dot_nt(wkt_ref[...], s)
        vtp_ref[...] = _dot_nt(wvt_ref[...], s)

    @pl.when(jnp.logical_not(prompt))
    def _():
        ks_ref[...] = k
        vs_ref[...] = v

    kb_ref[...] = k.astype(BF16)
    vb_ref[...] = v.astype(BF16)
    z = _dot(s, wf_ref[...]) + bf_ref[...]
    lf_ref[...] = -(jnp.maximum(-z, 0.0) + jnp.log1p(jnp.exp(-jnp.abs(z))))
    qn = _rms(h2, nb_ref[...]).astype(BF16)
    q_scale = jnp.where(prompt, qs_prompt, qs_sample)
    q_ref[...] = (_dot(qn, wq_ref[...]) * q_scale).astype(BF16)
    _by_slot(drain)


def _kvq_call(pos, h, slab, ys, nkv, wk, wv, wkt, wvt, wf, bf, nb, wq, p_tiles, tiles_per_seq, qs_prompt,
              qs_sample):
    d, hd = wk.shape
    c = d // LANES
    n = h.shape[0] // c
    tm = KVQ_TILE
    nt = n // tm
    np_ = p_tiles * tm
    tok = lambda w: pl.BlockSpec((tm, w), lambda i, p: (i, 0))
    cst = lambda a: pl.BlockSpec(a.shape, lambda i, p: (0,) * a.ndim, pipeline_mode=pl.Buffered(1))
    res = _tok_spec(tm, d, lambda i, p: (i, 0))

    def p_map(i, p):
        ip = jnp.minimum(i, p_tiles - 1)
        return (ip // tiles_per_seq, 0, ip % tiles_per_seq)

    p_out = pl.BlockSpec((None, hd, tm), p_map)
    s_out = pl.BlockSpec((tm, hd), lambda i, p: (jnp.maximum(i - p_tiles, 0), 0))
    kt_shape = jax.ShapeDtypeStruct((p_tiles // tiles_per_seq, hd, tiles_per_seq * tm), F32)
    grid_spec = pltpu.PrefetchScalarGridSpec(
        num_scalar_prefetch=1,
        grid=(nt,),
        in_specs=[res, tok(LANES), pl.BlockSpec(memory_space=pl.ANY),
                  cst(nkv), cst(wk), cst(wv), cst(wkt), cst(wvt), cst(wf), cst(bf), cst(nb), cst(wq)],
        out_specs=[res, p_out, p_out, s_out, s_out, tok(hd), tok(hd), tok(LANES), tok(hd)],
        scratch_shapes=[pltpu.VMEM((2, tm * c, LANES), F32), pltpu.VMEM((2, tm * c, LANES), F32),
                        pltpu.SemaphoreType.DMA((2,))],
    )
    return pl.pallas_call(
        functools.partial(_kvq_kernel, tm=tm, n=n, d=d, p_tiles=p_tiles, qs_prompt=qs_prompt,
                          qs_sample=qs_sample),
        grid_spec=grid_spec,
        out_shape=[jax.ShapeDtypeStruct(h.shape, F32), kt_shape, kt_shape,
                   jax.ShapeDtypeStruct((n - np_, hd), F32),
                   jax.ShapeDtypeStruct((n - np_, hd), F32), jax.ShapeDtypeStruct((n, hd), BF16),
                   jax.ShapeDtypeStruct((n, hd), BF16), jax.ShapeDtypeStruct((n, LANES), F32),
                   jax.ShapeDtypeStruct((n, hd), BF16)],
        compiler_params=_cparams(1),
        name="combine_kvq",
    )(pos, h, slab, ys, nkv, wk, wv, wkt, wvt, wf, bf, nb, wq)


def _final_kernel(pos_ref, h_ref, slab_ref, y_hbm, nfin_ref, yp_ref, ys_ref, ya_buf, yb_buf, sem,
                  *, tm, n, d, p_tiles):
    slot = pl.program_id(0) % 2
    fetch, drain = _fetch_pairs(pos_ref, y_hbm, ya_buf, yb_buf, sem, tm=tm, n=n, d=d)
    _by_slot(fetch)
    h4 = _combine(_tok_load(h_ref, tm, d), slab_ref[...],
                  _tok_load(ya_buf.at[slot], tm, d), _tok_load(yb_buf.at[slot], tm, d))
    y = _rms(h4, nfin_ref[...])
    prompt = pl.program_id(0) < p_tiles

    @pl.when(prompt)
    def _():
        yp_ref[...] = y

    @pl.when(jnp.logical_not(prompt))
    def _():
        ys_ref[...] = y

    _by_slot(drain)


def _final_call(pos, h, slab, ys, nfin, p_tiles):
    d = nfin.shape[1]
    c = d // LANES
    n = h.shape[0] // c
    tm = KVQ_TILE
    nt = n // tm
    np_ = p_tiles * tm
    grid_spec = pltpu.PrefetchScalarGridSpec(
        num_scalar_prefetch=1,
        grid=(nt,),
        in_specs=[_tok_spec(tm, d, lambda i, p: (i, 0)), pl.BlockSpec((tm, LANES), lambda i, p: (i, 0)),
                  pl.BlockSpec(memory_space=pl.ANY),
                  pl.BlockSpec(nfin.shape, lambda i, p: (0, 0), pipeline_mode=pl.Buffered(1))],
        out_specs=[pl.BlockSpec((tm, d), lambda i, p: (jnp.minimum(i, p_tiles - 1), 0)),
                   pl.BlockSpec((tm, d), lambda i, p: (jnp.maximum(i - p_tiles, 0), 0))],
        scratch_shapes=[pltpu.VMEM((2, tm * c, LANES), F32), pltpu.VMEM((2, tm * c, LANES), F32),
                        pltpu.SemaphoreType.DMA((2,))],
    )
    return pl.pallas_call(
        functools.partial(_final_kernel, tm=tm, n=n, d=d, p_tiles=p_tiles),
        grid_spec=grid_spec,
        out_shape=[jax.ShapeDtypeStruct((np_, d), F32), jax.ShapeDtypeStruct((n - np_, d), F32)],
        compiler_params=_cparams(1),
        name="combine_final",
    )(pos, h, slab, ys, nfin)


def _cumsum_kernel(x_ref, c_ref, *, t):
    rb = x_ref.shape[0]
    row = lax.broadcasted_iota(I32, (LANES, LANES), 0)
    col = lax.broadcasted_iota(I32, (LANES, LANES), 1)
    tri = jnp.where(row <= col, 1.0, 0.0).astype(BF16)
    carry = jnp.zeros((rb, 1), F32)
    for j in range(t // LANES):
        blk = x_ref[:, j * LANES:(j + 1) * LANES]
        a1 = blk.astype(BF16)
        r1 = blk - a1.astype(F32)
        a2 = r1.astype(BF16)
        a3 = (r1 - a2.astype(F32)).astype(BF16)
        cs = _dot(a1, tri) + _dot(a2, tri) + _dot(a3, tri) + carry
        c_ref[:, j * LANES:(j + 1) * LANES] = cs
        carry = cs[:, LANES - 1:LANES]


def _cumsum_call(x):
    r, t = x.shape
    rb = min(r, 128)
    return pl.pallas_call(
        functools.partial(_cumsum_kernel, t=t),
        grid=(r // rb,),
        in_specs=[pl.BlockSpec((rb, t), lambda i: (i, 0))],
        out_specs=pl.BlockSpec((rb, t), lambda i: (i, 0)),
        out_shape=jax.ShapeDtypeStruct((r, t), F32),
        compiler_params=_cparams(1),
        name="logf_cumsum",
    )(x)


def _bias_columns(c2, base, lane, query_side):
    p1 = c2.astype(BF16).astype(F32)
    r1 = c2 - p1
    p2 = r1.astype(BF16).astype(F32)
    p3 = r1 - p2
    off = lane - base
    if query_side:
        ones = (off >= 3) & (off < 6)
        return jnp.where(off == 0, p1, jnp.where(off == 1, p2, jnp.where(off == 2, p3, jnp.where(ones, 1.0, 0.0))))
    ones = (off >= 0) & (off < 3)
    return jnp.where(off == 3, -p1, jnp.where(off == 4, -p2, jnp.where(off == 5, -p3, jnp.where(ones, 1.0, 0.0))))


def _attn_prompt_kernel(q_ref, k_ref, v_ref, c_ref, o_ref, kx0, kx1, vx0, vx1, qx_buf, s_buf, p_buf, m_buf, acc_buf,
                        *, blk, dh, nkv, strip):
    hp = pl.program_id(1)
    qi = pl.program_id(2)
    kx = (kx0, kx1)
    vx = (vx0, vx1)
    lane = lax.broadcasted_iota(I32, (blk, 2 * dh), 1)
    in_head = [(lane >= hd_ * dh) & (lane < (hd_ + 1) * dh) for hd_ in range(2)]
    ext_base = [dh, 0]

    def column(cs, h):
        return jnp.sum(jnp.where(lane == h, cs, 0.0), axis=-1, keepdims=True) * LOG2E

    @pl.when(qi == 0)
    def _():
        for jc in range(nkv):
            rows = slice(jc * blk, (jc + 1) * blk)
            kb = k_ref[rows, :]
            vb = v_ref[rows, :]
            cs = c_ref[rows, :]
            for hd_ in range(2):
                ext = _bias_columns(column(cs, 2 * hp + hd_), ext_base[hd_], lane, False)
                kx[hd_][rows, :] = jnp.where(in_head[hd_], kb, ext.astype(BF16))
                vx[hd_][rows, :] = jnp.where(in_head[hd_], vb, jnp.ones_like(vb))

    q = q_ref[...]
    cs_q = c_ref[pl.ds(pl.multiple_of(qi * blk, blk), blk), :]
    for hd_ in range(2):
        ext = _bias_columns(column(cs_q, 2 * hp + hd_), ext_base[hd_], lane, True)
        qx_buf[hd_] = jnp.where(in_head[hd_], q, ext.astype(BF16))
        m_buf[hd_] = jnp.full((blk, 1), -jnp.inf, F32)
        acc_buf[hd_] = jnp.zeros((blk, 2 * dh), F32)

    def scores(j):
        start = pl.multiple_of(j * blk, blk)
        for hd_ in range(2):
            s_buf[hd_] = _dot_nt(qx_buf[hd_], kx[hd_][pl.ds(start, blk), :])

    def softmax(masked):
        for hd_ in range(2):
            for r0 in range(0, blk, strip):
                rows = slice(r0, r0 + strip)
                s = s_buf[hd_, rows, :]
                if masked:
                    row = lax.broadcasted_iota(I32, (strip, blk), 0) + r0
                    col = lax.broadcasted_iota(I32, (strip, blk), 1)
                    s = jnp.where(col <= row, s, -jnp.inf)
                m_old = m_buf[hd_, rows, :]
                m_new = jnp.maximum(m_old, jnp.max(s, axis=-1, keepdims=True))
                p_buf[hd_, rows, :] = jnp.exp2(s - m_new).astype(BF16)
                m_buf[hd_, rows, :] = m_new
                acc_buf[hd_, rows, :] = jnp.exp2(m_old - m_new) * acc_buf[hd_, rows, :]

    def values(j):
        start = pl.multiple_of(j * blk, blk)
        for hd_ in range(2):
            acc_buf[hd_] += _dot(p_buf[hd_], vx[hd_][pl.ds(start, blk), :])

    def body(j, carry):
        softmax(False)
        scores(j + 1)
        values(j)
        return carry

    scores(0)
    lax.fori_loop(0, qi, body, 0)
    softmax(True)
    values(qi)
    acc0 = acc_buf[0]
    acc1 = acc_buf[1]
    o0 = acc0 / pltpu.roll(acc0, dh, 1)
    o1 = acc1 / pltpu.roll(acc1, dh, 1)
    o_ref[...] = jnp.where(in_head[0], o0, o1).astype(BF16)


def _attn_prompt_call(q, kb, vb, c_slab, batch, seq, n_heads, dh):
    blk = min(ATTN_BLOCK, seq)
    nq = seq // blk
    hd = n_heads * dh
    pair = pl.BlockSpec((seq, 2 * dh), lambda b, hp, qi: (b, hp))
    return pl.pallas_call(
        functools.partial(_attn_prompt_kernel, blk=blk, dh=dh, nkv=nq, strip=min(ATTN_STRIP, blk)),
        grid=(batch, n_heads // 2, nq),
        in_specs=[pl.BlockSpec((blk, 2 * dh), lambda b, hp, qi: (b * nq + qi, hp)), pair, pair,
                  pl.BlockSpec((seq, LANES), lambda b, hp, qi: (b, 0))],
        out_specs=pl.BlockSpec((blk, 2 * dh), lambda b, hp, qi: (b * nq + qi, hp)),
        out_shape=jax.ShapeDtypeStruct((batch * seq, hd), BF16),
        scratch_shapes=[pltpu.VMEM((seq, 2 * dh), BF16)] * 4 + [
            pltpu.VMEM((2, blk, 2 * dh), BF16), pltpu.VMEM((2, blk, blk), F32), pltpu.VMEM((2, blk, blk), BF16),
            pltpu.VMEM((2, blk, 1), F32), pltpu.VMEM((2, blk, 2 * dh), F32)],
        compiler_params=_cparams(3),
        name="attn_prompt",
    )(q, kb, vb, c_slab)


def _attn_sample_kernel(q_ref, kc_ref, vc_ref, kn_ref, vn_ref, cq_ref, ck_ref, o_ref,
                        qbd, m_ref, l_ref, acc_ref, *, ts, n_heads, dh, nchunk):
    j = pl.program_id(1)
    r = n_heads * ts
    hd = n_heads * dh

    @pl.when(j == 0)
    def _():
        q = q_ref[...]
        qt = jnp.concatenate([q] * n_heads, axis=0)
        rr = lax.broadcasted_iota(I32, (r, hd), 0) // ts
        ll = lax.broadcasted_iota(I32, (r, hd), 1) // dh
        qbd[...] = jnp.where(rr == ll, qt, jnp.zeros_like(qt))
        m_ref[...] = jnp.full((r, 1), -jnp.inf, F32)
        l_ref[...] = jnp.zeros((r, 1), F32)
        acc_ref[...] = jnp.zeros((r, hd), F32)

    def expand(ck):
        w = ck.shape[1]
        return jnp.concatenate([jnp.broadcast_to(ck[h:h + 1, :], (ts, w)) for h in range(n_heads)], axis=0)

    def update(s, pv):
        m = m_ref[...]
        m_new = jnp.maximum(m, jnp.max(s, axis=-1, keepdims=True))
        p = jnp.exp(s - m_new)
        alpha = jnp.exp(m - m_new)
        l_ref[...] = alpha * l_ref[...] + jnp.sum(p, axis=-1, keepdims=True)
        acc_ref[...] = alpha * acc_ref[...] + pv(p.astype(BF16))
        m_ref[...] = m_new

    cq = cq_ref[...][:, 0:1]

    @pl.when(j < nchunk)
    def _():
        w = kc_ref.shape[-1]
        kt = kc_ref[...].reshape(hd, w).astype(BF16)
        vt = vc_ref[...].reshape(hd, w).astype(BF16)
        s = _dot(qbd[...], kt) + (cq - expand(ck_ref[...]))
        update(s, lambda p: _dot_nt(p, vt))

    @pl.when(j == nchunk)
    def _():
        s = _dot_nt(qbd[...], kn_ref[...]) + (cq - expand(ck_ref[:, 0:ts]))
        qpos = lax.broadcasted_iota(I32, (r, ts), 0) % ts
        kpos = lax.broadcasted_iota(I32, (r, ts), 1)
        s = jnp.where(kpos <= qpos, s, -jnp.inf)
        update(s, lambda p: _dot(p, vn_ref[...]))
        o = acc_ref[...] / l_ref[...]
        ll = lax.broadcasted_iota(I32, (ts, hd), 1) // dh
        out = jnp.zeros((ts, hd), F32)
        for h in range(n_heads):
            out = jnp.where(ll == h, o[h * ts:(h + 1) * ts, :], out)
        o_ref[...] = out.astype(BF16)


def _attn_sample_call(q, cache_kt, cache_vt, k_new, v_new, cq_rep, ck_chunks, row0, ts):
    n_streams, n_heads, dh, past = cache_kt.shape
    hd = n_heads * dh
    ck_len = min(CACHE_CHUNK, past)
    nchunk = past // ck_len
    r = n_heads * ts
    b0 = row0 // ts
    cache_spec = pl.BlockSpec((None, n_heads, dh, ck_len), lambda b, j: (b, 0, 0, jnp.minimum(j, nchunk - 1)))
    new_spec = pl.BlockSpec((ts, hd), lambda b, j: (b0 + b, 0))
    return pl.pallas_call(
        functools.partial(_attn_sample_kernel, ts=ts, n_heads=n_heads, dh=dh, nchunk=nchunk),
        grid=(n_streams, nchunk + 1),
        in_specs=[new_spec, cache_spec, cache_spec, new_spec, new_spec,
                  pl.BlockSpec((r, LANES), lambda b, j: (b, 0)),
                  pl.BlockSpec((n_heads, ck_len), lambda b, j: (b * (nchunk + 1) + j, 0))],
        out_specs=pl.BlockSpec((ts, hd), lambda b, j: (b, 0)),
        out_shape=jax.ShapeDtypeStruct((n_streams * ts, hd), BF16),
        scratch_shapes=[pltpu.VMEM((r, hd), BF16), pltpu.VMEM((r, 1), F32), pltpu.VMEM((r, 1), F32),
                        pltpu.VMEM((r, hd), F32)],
        compiler_params=_cparams(2),
        name="attn_sample",
    )(q, cache_kt, cache_vt, k_new, v_new, cq_rep, ck_chunks)


def _wo_kernel(h_ref, op_ref, os_ref, wo_ref, nf_ref, wr_ref, brt_ref, h3_ref, slab_ref, meta_ref, cnt_ref,
               carry_ref, *, tm, d, p_tiles, n_groups, epg):
    i = pl.program_id(0)

    @pl.when(i == 0)
    def _():
        carry_ref[...] = jnp.zeros(carry_ref.shape, F32)

    o = jnp.where(i >= p_tiles, os_ref[...], op_ref[...])
    h3 = _tok_load(h_ref, tm, d) + _dot(o, wo_ref[...])
    _tok_store(h3_ref, h3)
    slab_ref[...], meta_ref[...] = _route_tail(h3, nf_ref[...], wr_ref, brt_ref[...], carry_ref, n_groups, epg)
    cnt_ref[...] = carry_ref[...]


def _wo_call(h, o_p, o_s, wo, nf, wr, brt, n_groups, epg):
    hd, d = wo.shape
    n = h.shape[0] // (d // LANES)
    tm = TOKEN_TILE
    p_tiles = o_p.shape[0] // tm
    res = _tok_spec(tm, d, lambda i: (i, 0))
    return pl.pallas_call(
        functools.partial(_wo_kernel, tm=tm, d=d, p_tiles=p_tiles, n_groups=n_groups, epg=epg),
        grid=(n // tm,),
        in_specs=[res,
                  pl.BlockSpec((tm, hd), lambda i: (jnp.minimum(i, p_tiles - 1), 0)),
                  pl.BlockSpec((tm, hd), lambda i: (jnp.maximum(i - p_tiles, 0), 0)),
                  _const_spec(wo.shape), _const_spec(nf.shape), _const_spec(wr.shape), _const_spec(brt.shape)],
        out_specs=[res, pl.BlockSpec((tm, LANES), lambda i: (i, 0)),
                   pl.BlockSpec((SUBLANES, tm), lambda i: (i, 0)), pl.BlockSpec((1, LANES), lambda i: (0, 0))],
        out_shape=[jax.ShapeDtypeStruct(h.shape, F32), jax.ShapeDtypeStruct((n, LANES), F32),
                   jax.ShapeDtypeStruct((n // tm * SUBLANES, tm), F32), jax.ShapeDtypeStruct((1, LANES), F32)],
        scratch_shapes=[pltpu.VMEM((1, LANES), F32)],
        compiler_params=_cparams(1),
        name="wo_router",
    )(h, o_p, o_s, wo, nf, wr, brt)


def _router_weights(w_group, b_group, w_router, b_router):
    d = w_group.shape[0]
    n = w_group.shape[1] + w_router.shape[1]
    w = jnp.zeros((d, LANES), F32).at[:, :n].set(jnp.concatenate([w_group, w_router], axis=1))
    w1 = w.astype(BF16)
    w2 = (w - w1.astype(F32)).astype(BF16)
    b = jnp.zeros((1, LANES), F32).at[0, :n].set(jnp.concatenate([b_group, b_router]))
    return jnp.stack([w1, w2]), b


def _moe(h, meta, counts, nf, wg, wu, wd, layer, n_groups, n_experts):
    tile = ROUTE_TILE
    c = wg.shape[-2] // LANES
    n = h.shape[0] // c
    n_pairs = 2 * n
    cnt = counts[0, n_groups:n_groups + n_experts].astype(I32)
    padded = ((cnt + tile - 1) // tile) * tile
    ends = jnp.cumsum(padded)
    offs = ends - padded
    n_tiles = (n_pairs + n_experts * (tile - 1) + tile - 1) // tile
    n_valid = (ends[-1] // tile).astype(I32)
    starts = jnp.arange(n_tiles, dtype=I32) * tile
    te = jnp.zeros((n_tiles,), I32)
    for e in range(n_experts - 1):
        te = te + (starts >= ends[e]).astype(I32)
    te = jnp.where(jnp.arange(n_tiles) < n_valid, te, te[jnp.maximum(n_valid - 1, 0)])
    m = meta.reshape(-1, SUBLANES, meta.shape[1]).astype(I32)
    ids = m[:, 0:2, :].transpose(1, 0, 2).reshape(2, n)
    ranks = m[:, 4:6, :].transpose(1, 0, 2).reshape(2, n)
    base = jnp.zeros((2, n), I32)
    for e in range(n_experts):
        base = jnp.where(ids == e, offs[e], base)
    pos = (base + ranks).reshape(n_pairs) * c
    xs = _dispatch_call(pos, h, n_tiles * tile, c * LANES)
    ys = _experts_call(te, n_valid.reshape(1), xs, nf, wg, wu, wd, layer, n_tiles)
    return pos, ys


def kernel(x_prompt, x_sample, state_conv, cache_k, cache_v, cache_logf, norm_a, w_in_a, conv_w_a, w_out_a,
           norm_kv, w_k, w_v, w_f, b_f, norm_b, w_q_b, w_o_b, norm_ffn, w_group, b_group, w_router, b_router,
           w_gate, w_up, w_down, norm_final):
    bp, tp, d = x_prompt.shape
    bs, ts, _ = x_sample.shape
    past, n_heads, dh = cache_k.shape[1], cache_k.shape[2], cache_k.shape[3]
    hd = n_heads * dh
    n_groups = w_group.shape[-1]
    n_experts = w_gate.shape[1]
    epg = n_experts // n_groups
    np_, ns = bp * tp, bs * ts
    tm = TOKEN_TILE
    assert state_conv.shape[0] == 1 and w_q_b.shape[0] == 1 and state_conv.shape[2] == 2
    assert dh * 2 == LANES and n_heads % 2 == 0
    assert tp % tm == 0 and ns % tm == 0 and tm % ts == 0 and ts >= 2
    p_tiles = np_ // tm

    row = lambda a: a.reshape(1, -1).astype(F32)
    win = w_in_a[0].astype(BF16)
    wout = w_out_a[0].astype(BF16)
    wk, wv, wq, wo = w_k.astype(BF16), w_v.astype(BF16), w_q_b[0].astype(BF16), w_o_b[0].astype(BF16)
    wf = jnp.zeros((d, LANES), F32).at[:, :n_heads].set(w_f).astype(BF16)
    bf = jnp.zeros((1, LANES), F32).at[0, :n_heads].set(b_f)
    wr0, br0 = _router_weights(w_group[0], b_group[0], w_router[0], b_router[0])
    wr1, br1 = _router_weights(w_group[1], b_group[1], w_router[1], b_router[1])

    st = state_conv[0]
    s1 = jnp.zeros((bs, ts, d), F32).at[:, 0].set(st[:, 1]).reshape(ns, d)
    s2 = jnp.zeros((bs, ts, d), F32).at[:, 0].set(st[:, 0]).at[:, 1].set(st[:, 1]).reshape(ns, d)

    h1, slab0, meta0, tails, cus, cnt0 = _mixer_call(
        x_prompt.reshape(np_, d), x_sample.reshape(ns, d), s1, s2, tp, ts, row(norm_a[0]), win, conv_w_a[0], wout,
        row(norm_ffn[0]), wr0, br0, n_groups, epg)
    conv_prompt = tails[:p_tiles * SUBLANES].reshape(bp, tp // tm, SUBLANES, d)[:, -1, SUBLANES - 2:][None]
    conv_sample = cus.reshape(bs, ts, d)[:, ts - 2:][None]
    pos0, ys0 = _moe(h1, meta0, cnt0, row(norm_ffn[0]), w_gate, w_up, w_down, 0, n_groups, n_experts)

    h2, kt_p, vt_p, k_s, v_s, kb_all, vb_all, lf_all, q_all = _kvq_call(
        pos0, h1, slab0, ys0, row(norm_kv), wk, wv, w_k.T.astype(BF16), w_v.T.astype(BF16), wf, bf, row(norm_b[0]), wq,
        np_ // KVQ_TILE, tp // KVQ_TILE, float(dh) ** -0.5 * LOG2E, float(dh) ** -0.5)
    k_prompt = kt_p.reshape(bp, n_heads, dh, tp).transpose(0, 3, 1, 2)
    v_prompt = vt_p.reshape(bp, n_heads, dh, tp).transpose(0, 3, 1, 2)
    logf_prompt = lf_all[:np_, :n_heads].reshape(bp, tp, n_heads)
    logf_sample = lf_all[np_:, :n_heads].reshape(bs, ts, n_heads)

    c_p = _cumsum_call(logf_prompt.transpose(0, 2, 1).reshape(bp * n_heads, tp))
    c_slab = jnp.zeros((np_, LANES), F32).at[:, :n_heads].set(
        c_p.reshape(bp, n_heads, tp).transpose(0, 2, 1).reshape(np_, n_heads))
    o_p = _attn_prompt_call(q_all, kb_all, vb_all, c_slab, bp, tp, n_heads, dh)

    tall = past + ts
    tpad = ((tall + LANES - 1) // LANES) * LANES
    lfs = jnp.concatenate([cache_logf.astype(F32), logf_sample], axis=1).transpose(0, 2, 1)
    lfs = jnp.pad(lfs, ((0, 0), (0, 0), (0, tpad - tall))).reshape(bs * n_heads, tpad)
    c_s = _cumsum_call(lfs).reshape(bs, n_heads, tpad)
    ck_len = min(CACHE_CHUNK, past)
    nchunk = past // ck_len
    c_past = c_s[:, :, :past].reshape(bs, n_heads, nchunk, ck_len).transpose(0, 2, 1, 3)
    c_new = c_s[:, :, past:past + ts]
    c_new_pad = jnp.pad(c_new, ((0, 0), (0, 0), (0, ck_len - ts)))[:, None]
    ck_s = jnp.concatenate([c_past, c_new_pad], axis=1).reshape(bs * (nchunk + 1) * n_heads, ck_len)
    cq_s = jnp.broadcast_to(c_new.reshape(bs * n_heads * ts, 1), (bs * n_heads * ts, LANES))
    o_s = _attn_sample_call(q_all, cache_k.transpose(0, 2, 3, 1), cache_v.transpose(0, 2, 3, 1), kb_all, vb_all,
                            cq_s, ck_s, np_, ts)

    h3, slab1, meta1, cnt1 = _wo_call(h2, o_p, o_s, wo, row(norm_ffn[1]), wr1, br1, n_groups, epg)
    pos1, ys1 = _moe(h3, meta1, cnt1, row(norm_ffn[1]), w_gate, w_up, w_down, 1, n_groups, n_experts)
    y_p, y_s = _final_call(pos1, h3, slab1, ys1, row(norm_final), np_ // KVQ_TILE)

    return (y_p.reshape(bp, tp, d), y_s.reshape(bs, ts, d), conv_prompt, conv_sample,
            k_prompt, v_prompt, logf_prompt,
            k_s.reshape(bs, ts, n_heads, dh), v_s.reshape(bs, ts, n_heads, dh), logf_sample)
```
